```python
import math
import jax, jax.numpy as jnp
from jax import lax
import numpy as np

D_MODEL = 1024
BATCH = 8
SEQ = 4096
DEPTH = 2

GROUP_WIDTH = D_MODEL // 4
D_MIX = 4 * GROUP_WIDTH
EPS = 1e-6

GLA_HEADS = 4
GLA_DK = GROUP_WIDTH // (2 * GLA_HEADS)
GLA_DV = GROUP_WIDTH // GLA_HEADS
GLA_GATE_RANK = 16
GLA_GATE_NORM = 16.0
GLA_CHUNK = 64

DIFF_HEADS = 4
DIFF_DH = GROUP_WIDTH // (2 * DIFF_HEADS)
DIFF_DV = 2 * DIFF_DH
DIFF_BLOCK = 128
ROPE_THETA = 10000.0

S5_CH = 16
S5_GROUPS = GROUP_WIDTH // S5_CH
S5_STATE = 64

LRU_WIDTH = GROUP_WIDTH
LRU_BLOCKS = 4
LRU_BLOCK = LRU_WIDTH // LRU_BLOCKS
LRU_CONV = 4
LRU_C = 8.0

D_FF = 4 * D_MODEL

SPLITS = (GLA_HEADS * GLA_DK, GLA_HEADS * GLA_DK, GLA_HEADS * GLA_DV, GLA_GATE_RANK, GLA_HEADS * GLA_DV,
          DIFF_HEADS * 2 * DIFF_DH, DIFF_HEADS * 2 * DIFF_DH, DIFF_HEADS * DIFF_DV,
          GROUP_WIDTH,
          LRU_WIDTH, LRU_WIDTH)
D_IN = sum(SPLITS)

kernel_name = "hybrid_parallel_gla_diff_s5_rglru"


def _rmsnorm(x, g):
    xf = x.astype(jnp.float32)
    y = xf * lax.rsqrt(jnp.mean(xf * xf, axis=-1, keepdims=True) + EPS) * g.astype(jnp.float32)
    return y.astype(x.dtype)


def _split_cols(p):
    offs = [int(o) for o in np.cumsum(SPLITS)[:-1]]
    return jnp.split(p, offs, axis=-1)


def _lin_combine(e1, e2):
    a1, b1 = e1
    a2, b2 = e2
    return (a2 * a1, a2 * b1 + b2)


def _gla(q, k, v, glr, og, w_gate, b_gate, norm_g):
    f32 = jnp.float32
    B, S, _ = q.shape
    H, DK, DV, C = GLA_HEADS, GLA_DK, GLA_DV, GLA_CHUNK
    NC = S // C

    def chunks(t, d):
        return t.astype(f32).reshape(B, NC, C, H, d).transpose(0, 3, 1, 2, 4)

    qc = chunks(q, DK) * (DK ** -0.5)
    kc = chunks(k, DK)
    vc = chunks(v, DV)
    g = jax.nn.log_sigmoid(glr.astype(f32) @ w_gate.astype(f32) + b_gate.astype(f32)) / GLA_GATE_NORM
    gc = g.reshape(B, NC, C, H, DK).transpose(0, 3, 1, 2, 4)
    bcum = jnp.cumsum(gc, axis=3)
    blast = bcum[:, :, :, -1:, :]
    q_dec = qc * jnp.exp(bcum)
    k_dec = kc * jnp.exp(-bcum)
    k_st = kc * jnp.exp(blast - bcum)
    mask = jnp.tril(jnp.ones((C, C), dtype=bool))
    attn = jnp.where(mask, jnp.einsum('bhncd,bhnjd->bhncj', q_dec, k_dec), 0.0)
    o = jnp.einsum('bhncj,bhnjv->bhncv', attn, vc)
    kv = jnp.einsum('bhncd,bhncv->bhndv', k_st, vc)
    dec = jnp.exp(blast[:, :, :, 0, :])

    def step(state, inp):
        d, kvn = inp
        return d[..., None] * state + kvn, state

    _, s_prev = lax.scan(step, jnp.zeros((B, H, DK, DV), f32),
                         (jnp.moveaxis(dec, 2, 0), jnp.moveaxis(kv, 2, 0)))
    s_prev = jnp.moveaxis(s_prev, 0, 2)
    o = o + jnp.einsum('bhncd,bhndv->bhncv', q_dec, s_prev)
    o = o.transpose(0, 2, 3, 1, 4).reshape(B, S, H, DV)
    o = _rmsnorm(o, norm_g) * jax.nn.silu(og.astype(f32).reshape(B, S, H, DV))
    return o.reshape(B, S, H * DV)


def _rope_tables(S, dh):
    inv = ROPE_THETA ** (-jnp.arange(0, dh, 2, dtype=jnp.float32) / dh)
    ang = jnp.arange(S, dtype=jnp.float32)[:, None] * inv[None, :]
    emb = jnp.concatenate([ang, ang], axis=-1)
    return jnp.cos(emb), jnp.sin(emb)


def _rope(x, cos, sin):
    c = cos[None, :, None, None, :].astype(x.dtype)
    s = sin[None, :, None, None, :].astype(x.dtype)
    x1, x2 = jnp.split(x, 2, axis=-1)
    return x * c + jnp.concatenate([-x2, x1], axis=-1) * s


def _diff_attn(q, k, v, lq1, lk1, lq2, lk2, norm_g, lam_init):
    f32 = jnp.float32
    B, S, _ = q.shape
    H, DH, DV, BLK = DIFF_HEADS, DIFF_DH, DIFF_DV, DIFF_BLOCK
    NB = S // BLK
    cos, sin = _rope_tables(S, DH)
    q = _rope(q.reshape(B, S, H, 2, DH), cos, sin) * (DH ** -0.5)
    k = _rope(k.reshape(B, S, H, 2, DH), cos, sin)
    v = v.reshape(B, S, H, DV)
    lam = (jnp.exp(jnp.sum(lq1.astype(f32) * lk1.astype(f32)))
           - jnp.exp(jnp.sum(lq2.astype(f32) * lk2.astype(f32))) + lam_init)
    qb = q.reshape(B, NB, BLK, H, 2, DH).transpose(1, 0, 2, 3, 4, 5)
    kpos = jnp.arange(S)

    def block(args):
        q_blk, i = args
        s = jnp.einsum('bqhcd,bkhcd->bhcqk', q_blk, k).astype(f32)
        qpos = i * BLK + jnp.arange(BLK)
        mask = kpos[None, :] <= qpos[:, None]
        p = jax.nn.softmax(jnp.where(mask, s, -jnp.inf), axis=-1)
        w = p[:, :, 0] - lam * p[:, :, 1]
        return jnp.einsum('bhqk,bkhv->bqhv', w.astype(v.dtype), v)

    o = lax.map(block, (qb, jnp.arange(NB)))
    o = o.transpose(1, 0, 2, 3, 4).reshape(B, S, H, DV)
    o = _rmsnorm(o, norm_g) * (1.0 - lam_init)
    return o.reshape(B, S, H * DV)


def _s5(u, log_step, a_re, a_im, b_re, b_im, c_re, c_im, d, w_glu, b_glu):
    f32 = jnp.float32
    B, S, W = u.shape
    G, P, CH = S5_GROUPS, S5_STATE, S5_CH
    uf = u.astype(f32)
    ug = uf.reshape(B, S, G, CH)
    step = jnp.exp(log_step.astype(f32))[:, None]
    lr, li = a_re.astype(f32), a_im.astype(f32)
    mag = jnp.exp(lr * step)
    ab_re = mag * jnp.cos(li * step)
    ab_im = mag * jnp.sin(li * step)
    den = lr * lr + li * li
    nr, ni = ab_re - 1.0, ab_im
    cr = (nr * lr + ni * li) / den
    ci = (ni * lr - nr * li) / den
    br, bi = b_re.astype(f32), b_im.astype(f32)
    bb_re = cr[..., None] * br - ci[..., None] * bi
    bb_im = cr[..., None] * bi + ci[..., None] * br
    bu_re = jnp.einsum('bsgh,gph->bsgp', ug, bb_re)
    bu_im = jnp.einsum('bsgh,gph->bsgp', ug, bb_im)
    ar = jnp.broadcast_to(ab_re[None, None], (1, S, G, P))
    ai = jnp.broadcast_to(ab_im[None, None], (1, S, G, P))

    def combine(e1, e2):
        a1r, a1i, b1r, b1i = e1
        a2r, a2i, b2r, b2i = e2
        return (a2r * a1r - a2i * a1i, a2r * a1i + a2i * a1r,
                a2r * b1r - a2i * b1i + b2r, a2r * b1i + a2i * b1r + b2i)

    _, _, xr, xi = lax.associative_scan(combine, (ar, ai, bu_re, bu_im), axis=1)
    y = (jnp.einsum('gqp,bsgp->bsgq', c_re.astype(f32), xr)
         - jnp.einsum('gqp,bsgp->bsgq', c_im.astype(f32), xi))
    y = y.reshape(B, S, W) + d.astype(f32) * uf
    y = jax.nn.gelu(y)
    return y * jax.nn.sigmoid(y @ w_glu.astype(f32) + b_glu.astype(f32))


def _rglru(xb, gate, conv_w, conv_b, w_a, b_a, w_x, b_x, lam):
    f32 = jnp.float32
    B, S, W = xb.shape
    xc = lax.conv_general_dilated(xb.astype(f32), conv_w.astype(f32)[:, None, :],
                                  window_strides=(1,), padding=[(LRU_CONV - 1, 0)],
                                  dimension_numbers=('NWC', 'WIO', 'NWC'),
                                  feature_group_count=W) + conv_b.astype(f32)
    xr = xc.reshape(B, S, LRU_BLOCKS, LRU_BLOCK)
    r = jax.nn.sigmoid(jnp.einsum('bsnc,ncd->bsnd', xr, w_a.astype(f32)).reshape(B, S, W) + b_a.astype(f32))
    i = jax.nn.sigmoid(jnp.einsum('bsnc,ncd->bsnd', xr, w_x.astype(f32)).reshape(B, S, W) + b_x.astype(f32))
    log_a = -LRU_C * r * jax.nn.softplus(-lam.astype(f32))
    a = jnp.exp(log_a)
    mult = jnp.sqrt(jnp.maximum(-jnp.expm1(2.0 * log_a), 1e-12))
    _, h = lax.associative_scan(_lin_combine, (a, mult * (i * xc)), axis=1)
    return h * jax.nn.gelu(gate.astype(f32))


def setup_inputs(seed: int = 0) -> dict:
    key = jax.random.key(seed)
    ks = iter(jax.random.split(key, 48))
    f32 = jnp.float32
    L = DEPTH

    def nrm(shape, scale):
        return jax.random.normal(next(ks), shape, f32) * scale

    def gain(shape):
        return 1.0 + 0.01 * jax.random.normal(next(ks), shape, f32)

    x = nrm((BATCH, SEQ, D_MODEL), 1.0)
    ln_mix_pre = gain((L, D_MODEL))
    ln_mix_post = gain((L, D_MODEL))
    ln_ffn_pre = gain((L, D_MODEL))
    ln_ffn_post = gain((L, D_MODEL))
    w_in = nrm((L, D_MODEL, D_IN), D_MODEL ** -0.5)
    w_out = nrm((L, D_MIX, D_MODEL), D_MIX ** -0.5)
    gla_w_gate = nrm((L, GLA_GATE_RANK, GLA_HEADS * GLA_DK), GLA_GATE_RANK ** -0.5)
    gla_b_gate = nrm((L, GLA_HEADS * GLA_DK), 0.01)
    gla_norm = gain((L, GLA_DV))
    diff_lq1 = nrm((L, DIFF_DH), 0.1)
    diff_lk1 = nrm((L, DIFF_DH), 0.1)
    diff_lq2 = nrm((L, DIFF_DH), 0.1)
    diff_lk2 = nrm((L, DIFF_DH), 0.1)
    diff_norm = gain((L, DIFF_DV))
    s5_log_step = jax.random.uniform(next(ks), (L, S5_GROUPS), f32,
                                     minval=math.log(1e-3), maxval=math.log(1e-1))
    s5_a_re = -0.5 + nrm((L, S5_GROUPS, S5_STATE), 0.01)
    s5_a_im = math.pi * jnp.arange(S5_STATE, dtype=f32)[None, None, :] + nrm((L, S5_GROUPS, S5_STATE), 0.01)
    s5_b_re = nrm((L, S5_GROUPS, S5_STATE, S5_CH), (2 * S5_CH) ** -0.5)
    s5_b_im = nrm((L, S5_GROUPS, S5_STATE, S5_CH), (2 * S5_CH) ** -0.5)
    s5_c_re = nrm((L, S5_GROUPS, S5_CH, S5_STATE), (2 * S5_STATE) ** -0.5)
    s5_c_im = nrm((L, S5_GROUPS, S5_CH, S5_STATE), (2 * S5_STATE) ** -0.5)
    s5_d = nrm((L, GROUP_WIDTH), 1.0)
    s5_w_glu = nrm((L, GROUP_WIDTH, GROUP_WIDTH), GROUP_WIDTH ** -0.5)
    s5_b_glu = nrm((L, GROUP_WIDTH), 0.01)
    lru_conv_w = nrm((L, LRU_CONV, LRU_WIDTH), LRU_CONV ** -0.5)
    lru_conv_b = nrm((L, LRU_WIDTH), 0.01)
    lru_w_a = nrm((L, LRU_BLOCKS, LRU_BLOCK, LRU_BLOCK), LRU_BLOCK ** -0.5)
    lru_b_a = nrm((L, LRU_WIDTH), 0.01)
    lru_w_x = nrm((L, LRU_BLOCKS, LRU_BLOCK, LRU_BLOCK), LRU_BLOCK ** -0.5)
    lru_b_x = nrm((L, LRU_WIDTH), 0.01)
    a_pow = jax.random.uniform(next(ks), (L, LRU_WIDTH), f32, minval=0.9, maxval=0.999)
    a0 = a_pow ** (1.0 / LRU_C)
    lru_lambda = jnp.log(a0) - jnp.log1p(-a0)
    ffn_w1 = nrm((L, D_MODEL, D_FF), D_MODEL ** -0.5)
    ffn_w2 = nrm((L, D_FF, D_MODEL), D_FF ** -0.5)
    return {"x": x, "ln_mix_pre": ln_mix_pre, "ln_mix_post": ln_mix_post,
            "ln_ffn_pre": ln_ffn_pre, "ln_ffn_post": ln_ffn_post,
            "w_in": w_in, "w_out": w_out,
            "gla_w_gate": gla_w_gate, "gla_b_gate": gla_b_gate, "gla_norm": gla_norm,
            "diff_lq1": diff_lq1, "diff_lk1": diff_lk1, "diff_lq2": diff_lq2, "diff_lk2": diff_lk2,
            "diff_norm": diff_norm,
            "s5_log_step": s5_log_step, "s5_a_re": s5_a_re, "s5_a_im": s5_a_im,
            "s5_b_re": s5_b_re, "s5_b_im": s5_b_im, "s5_c_re": s5_c_re, "s5_c_im": s5_c_im,
            "s5_d": s5_d, "s5_w_glu": s5_w_glu, "s5_b_glu": s5_b_glu,
            "lru_conv_w": lru_conv_w, "lru_conv_b": lru_conv_b, "lru_w_a": lru_w_a, "lru_b_a": lru_b_a,
            "lru_w_x": lru_w_x, "lru_b_x": lru_b_x, "lru_lambda": lru_lambda,
            "ffn_w1": ffn_w1, "ffn_w2": ffn_w2}


def reference(x, ln_mix_pre, ln_mix_post, ln_ffn_pre, ln_ffn_post, w_in, w_out,
              gla_w_gate, gla_b_gate, gla_norm,
              diff_lq1, diff_lk1, diff_lq2, diff_lk2, diff_norm,
              s5_log_step, s5_a_re, s5_a_im, s5_b_re, s5_b_im, s5_c_re, s5_c_im,
              s5_d, s5_w_glu, s5_b_glu,
              lru_conv_w, lru_conv_b, lru_w_a, lru_b_a, lru_w_x, lru_b_x, lru_lambda,
              ffn_w1, ffn_w2):
    for l in range(DEPTH):
        lam_init = 0.8 - 0.6 * math.exp(-0.3 * l)
        h = _rmsnorm(x, ln_mix_pre[l])
        proj = h @ w_in[l]
        (g_q, g_k, g_v, g_lr, g_og, d_q, d_k, d_v, s_u, r_x, r_g) = _split_cols(proj)
        o_a = _gla(g_q, g_k, g_v, g_lr, g_og, gla_w_gate[l], gla_b_gate[l], gla_norm[l])
        o_b = _diff_attn(d_q, d_k, d_v, diff_lq1[l], diff_lk1[l], diff_lq2[l], diff_lk2[l],
                         diff_norm[l], lam_init)
        o_c = _s5(s_u, s5_log_step[l], s5_a_re[l], s5_a_im[l], s5_b_re[l], s5_b_im[l],
                  s5_c_re[l], s5_c_im[l], s5_d[l], s5_w_glu[l], s5_b_glu[l])
        o_d = _rglru(r_x, r_g, lru_conv_w[l], lru_conv_b[l], lru_w_a[l], lru_b_a[l],
                     lru_w_x[l], lru_b_x[l], lru_lambda[l])
        mix = jnp.concatenate([o_a.astype(x.dtype), o_b.astype(x.dtype),
                               o_c.astype(x.dtype), o_d.astype(x.dtype)], axis=-1) @ w_out[l]
        x = x + _rmsnorm(mix, ln_mix_post[l])
        h = _rmsnorm(x, ln_ffn_pre[l])
        f = jnp.square(jax.nn.relu(h @ ffn_w1[l])) @ ffn_w2[l]
        x = x + _rmsnorm(f, ln_ffn_post[l])
    return x
```

```python
import functools
import math

import numpy as np
import jax
import jax.numpy as jnp
from jax import lax
from jax.experimental import pallas as pl
from jax.experimental.pallas import tpu as pltpu

F32 = jnp.float32
BF16 = jnp.bfloat16

D_MODEL = 1024
GROUP_WIDTH = 256
EPS = 1e-6
GLA_HEADS, GLA_DK, GLA_DV, GLA_RANK, GLA_CHUNK = 4, 32, 64, 16, 64
GLA_GATE_NORM = 16.0
DIFF_HEADS, DIFF_DH, DIFF_DV = 4, 32, 64
ROPE_THETA = 10000.0
S5_GROUPS, S5_STATE, S5_CH = 16, 64, 16
S5_LANES = S5_GROUPS * S5_STATE
LRU_BLOCKS, LRU_BLOCK, LRU_CONV, LRU_C = 4, 64, 4, 8.0
D_FF = 4 * D_MODEL

SUBLANES = 8
LANES = 128
VMEM_LIMIT = 56 * 1024 * 1024

TM_INPROJ = 512
T_ATT = 256
LT_GLA = 256
LC_S5 = 256
LT_LRU = 256
TM_FFN = 512
FF_CHUNK = 1024


def _dot(a, b):
    return jnp.dot(a, b, preferred_element_type=F32)


def _dot_nt(a, b):
    return lax.dot_general(a, b, (((1,), (1,)), ((), ())), preferred_element_type=F32)


def _dot_tn(a, b):
    return lax.dot_general(a, b, (((0,), (0,)), ((), ())), preferred_element_type=F32)


def _split2(x):
    hi = x.astype(BF16)
    lo = (x - hi.astype(F32)).astype(BF16)
    return hi, lo


def _split3(x):
    hi = x.astype(BF16)
    r = x - hi.astype(F32)
    mid = r.astype(BF16)
    lo = (r - mid.astype(F32)).astype(BF16)
    return hi, mid, lo


def _sigmoid(x):
    return 1.0 / (1.0 + jnp.exp(-x))


def _softplus(x):
    return jnp.maximum(x, 0.0) + jnp.log1p(jnp.exp(-jnp.abs(x)))


def _gelu_tanh(x):
    c = math.sqrt(2.0 / math.pi)
    return 0.5 * x * (1.0 + jnp.tanh(c * (x + 0.044715 * (x * x * x))))


def _rms(x, g):
    ms = jnp.mean(x * x, axis=-1, keepdims=True)
    return x * lax.rsqrt(ms + EPS) * g


def _params(*sem):
    return pltpu.CompilerParams(dimension_semantics=sem, vmem_limit_bytes=VMEM_LIMIT)


def _const_spec(shape):
    nd = len(shape)
    return pl.BlockSpec(shape, lambda *_: (0,) * nd)


N_GLA = 896
N_NAT = N_GLA + 5 * GROUP_WIDTH
N_T = 3 * GROUP_WIDTH


def _inproj_kernel(x_ref, g_ref, wn_ref, wt_ref, cn_ref, sn_ref, ct_ref, st_ref,
                   gla_ref, kd_ref, qt_ref, vt_ref, su_ref, rxg_ref):
    h = _rms(x_ref[...], g_ref[...]).astype(BF16)
    W = GROUP_WIDTH
    gla_ref[...] = _dot(h, wn_ref[:, :N_GLA])
    o = N_GLA
    k = _dot(h, wn_ref[:, o:o + W])
    kr = _dot(h, wn_ref[:, o + W:o + 2 * W])
    kd_ref[...] = (k * cn_ref[...] + kr * sn_ref[...]).astype(BF16)
    su_ref[...] = _dot(h, wn_ref[:, o + 2 * W:o + 3 * W])
    rxg_ref[...] = _dot(h, wn_ref[:, o + 3 * W:o + 5 * W])
    q = _dot_nt(wt_ref[:W, :], h)
    qr = _dot_nt(wt_ref[W:2 * W, :], h)
    qt_ref[...] = (q * ct_ref[...] + qr * st_ref[...]).astype(BF16)
    vt = _dot_nt(wt_ref[2 * W:, :], h).astype(BF16)
    for j in range(vt_ref.shape[0]):
        vt_ref[j] = vt[:, j * T_ATT:(j + 1) * T_ATT]


def _inproj(x, g, wn, wt, cn, sn, ct, st):
    B, S, D = x.shape
    tm = TM_INPROJ
    W = GROUP_WIDTH
    nsub = tm // T_ATT
    out_shape = (
        jax.ShapeDtypeStruct((B, S, N_GLA), F32),
        jax.ShapeDtypeStruct((B, S, W), BF16),
        jax.ShapeDtypeStruct((B, W, S), BF16),
        jax.ShapeDtypeStruct((B, S // T_ATT, W, T_ATT), BF16),
        jax.ShapeDtypeStruct((B, S, W), F32),
        jax.ShapeDtypeStruct((B, S, 2 * W), F32),
    )
    row = lambda n: pl.BlockSpec((None, tm, n), lambda b, t: (b, t, 0))
    return pl.pallas_call(
        _inproj_kernel,
        out_shape=out_shape,
        grid=(B, S // tm),
        in_specs=[
            row(D),
            _const_spec((1, D)),
            _const_spec((D, N_NAT)),
            _const_spec((N_T, D)),
            pl.BlockSpec((tm, W), lambda b, t: (t, 0)),
            pl.BlockSpec((tm, W), lambda b, t: (t, 0)),
            pl.BlockSpec((W, tm), lambda b, t: (0, t)),
            pl.BlockSpec((W, tm), lambda b, t: (0, t)),
        ],
        out_specs=(
            row(N_GLA),
            row(W),
            pl.BlockSpec((None, W, tm), lambda b, t: (b, 0, t)),
            pl.BlockSpec((None, nsub, W, T_ATT), lambda b, t: (b, t, 0, 0)),
            row(W),
            row(2 * W),
        ),
        compiler_params=_params("parallel", "parallel"),
        name="inproj",
    )(x, g, wn, wt, cn, sn, ct, st)


def _gla_kernel(p_ref, wg_ref, bg_ref, tril_ref, gm_ref, gain_ref, o_ref, st_ref):
    Lt = p_ref.shape[0]
    C = GLA_CHUNK
    HK = GLA_HEADS * GLA_DK
    HV = GLA_HEADS * GLA_DV

    @pl.when(pl.program_id(1) == 0)
    def _():
        st_ref[...] = jnp.zeros_like(st_ref)

    q = p_ref[:, 0:HK] * (GLA_DK ** -0.5)
    k = p_ref[:, HK:2 * HK]
    v = p_ref[:, 2 * HK:2 * HK + HV].astype(BF16)
    og = p_ref[:, 2 * HK + HV:2 * HK + 2 * HV]
    glr = p_ref[:, 2 * HK + 2 * HV:].astype(BF16)

    g = -_softplus(-(_dot(glr, wg_ref[...]) + bg_ref[...])) * (1.0 / GLA_GATE_NORM)
    g_hi, g_lo = _split2(g)
    tril = tril_ref[...]
    bcum = _dot(tril, g_hi) + _dot(tril, g_lo)
    q_dec = (q * jnp.exp(bcum)).astype(BF16)
    k_dec = (k * jnp.exp(-bcum)).astype(BF16)

    lane_k = lax.broadcasted_iota(jnp.int32, (C, HK), 1) // GLA_DK
    r_a = lax.broadcasted_iota(jnp.int32, (GLA_HEADS * C, C), 0) % C
    c_a = lax.broadcasted_iota(jnp.int32, (GLA_HEADS * C, C), 1)
    causal = r_a >= c_a
    lane_v = lax.broadcasted_iota(jnp.int32, (C, HV), 1) // GLA_DV
    st_row = lax.broadcasted_iota(jnp.int32, (HV, HK), 0) // GLA_DV
    st_col = lax.broadcasted_iota(jnp.int32, (HV, HK), 1) // GLA_DK
    same_head = st_row == st_col

    outs = []
    for c in range(Lt // C):
        sl = slice(c * C, (c + 1) * C)
        qd, kd, vc = q_dec[sl], k_dec[sl], v[sl]
        bc = bcum[sl]
        blast = bc[C - 1:C, :]
        k_st = (k[sl] * jnp.exp(blast - bc)).astype(BF16)
        q_big = jnp.concatenate(
            [jnp.where(lane_k == h, qd, jnp.zeros_like(qd)) for h in range(GLA_HEADS)], axis=0)
        attn = jnp.where(causal, _dot_nt(q_big, kd), 0.0).astype(BF16)
        o_all = _dot(attn, vc)
        o_c = jnp.zeros((C, HV), F32)
        for h in range(GLA_HEADS):
            o_c = o_c + jnp.where(lane_v == h, o_all[h * C:(h + 1) * C], 0.0)
        state = st_ref[...]
        o_c = o_c + _dot_nt(qd, state.astype(BF16))
        kv_t = jnp.where(same_head, _dot_tn(vc, k_st), 0.0)
        st_ref[...] = state * jnp.exp(blast) + kv_t
        outs.append(o_c)
    o = jnp.concatenate(outs, axis=0)
    sq_hi, sq_lo = _split2(o * o)
    ms = _dot(sq_hi, gm_ref[...]) + _dot(sq_lo, gm_ref[...])
    o = o * lax.rsqrt(ms + EPS) * gain_ref[...]
    o_ref[...] = (o * (og * _sigmoid(og))).astype(BF16)


def _gla(p, wg, bg, tril, gm, gain):
    B, S, _ = p.shape
    Lt = LT_GLA
    HV = GLA_HEADS * GLA_DV
    HK = GLA_HEADS * GLA_DK
    return pl.pallas_call(
        _gla_kernel,
        out_shape=jax.ShapeDtypeStruct((B, S, HV), BF16),
        grid=(B, S // Lt),
        in_specs=[
            pl.BlockSpec((None, Lt, N_GLA), lambda b, t: (b, t, 0)),
            _const_spec((HK, HK)),
            _const_spec((1, HK)),
            _const_spec((Lt, Lt)),
            _const_spec((HV, HV)),
            _const_spec((1, HV)),
        ],
        out_specs=pl.BlockSpec((None, Lt, HV), lambda b, t: (b, t, 0)),
        scratch_shapes=[pltpu.VMEM((HV, HK), F32)],
        compiler_params=_params("parallel", "arbitrary"),
        name="gla",
    )(p, wg, bg, tril, gm, gain)


V_AUG = DIFF_DV + 16


def _diff_kernel(qt_ref, k_ref, vt_ref, lq_ref, lk_ref, gain_ref, o_ref, *, lam_init):
    T = T_ATT
    i = pl.program_id(1)
    qt = qt_ref[...]
    row = lax.broadcasted_iota(jnp.int32, qt.shape, 0) // DIFF_DH
    e = jnp.exp(jnp.sum(lq_ref[...] * lk_ref[...], axis=-1, keepdims=True))
    lam = e[0:1] - e[1:2] + lam_init
    kk = lax.broadcasted_iota(jnp.int32, (T, T), 0)
    qq = lax.broadcasted_iota(jnp.int32, (T, T), 1)
    causal = kk <= qq
    ones = jnp.ones((V_AUG - DIFF_DV, T), BF16)

    def v_aug(j, h):
        return jnp.concatenate([vt_ref[j, h * DIFF_DV:(h + 1) * DIFF_DV, :], ones], axis=0)

    heads = []
    for h in range(DIFF_HEADS):
        wq = jnp.concatenate(
            [jnp.where(row == 2 * h + c, qt, jnp.zeros_like(qt)) for c in range(2)], axis=1)

        kd = k_ref[pl.ds(pl.multiple_of(i * T, T), T), :]
        s = _dot(kd, wq)
        va = v_aug(i, h)
        carry = []
        for c in range(2):
            sc = jnp.where(causal, s[:, c * T:(c + 1) * T], -1e30)
            m = jnp.max(sc, axis=0, keepdims=True)
            p = jnp.exp(sc - m).astype(BF16)
            carry += [m, _dot(va, p)]

        def body(j, carry):
            kb = k_ref[pl.ds(pl.multiple_of(j * T, T), T), :]
            s = _dot(kb, wq)
            va = v_aug(j, h)
            new = []
            for c in range(2):
                m, acc = carry[2 * c], carry[2 * c + 1]
                sc = s[:, c * T:(c + 1) * T]
                m_new = jnp.maximum(m, jnp.max(sc, axis=0, keepdims=True))
                alpha = jnp.exp(m - m_new)
                p = jnp.exp(sc - m_new).astype(BF16)
                new += [m_new, alpha * acc + _dot(va, p)]
            return tuple(new)

        _, a1, _, a2 = lax.fori_loop(0, i, body, tuple(carry))
        o1 = a1[:DIFF_DV] / a1[DIFF_DV:DIFF_DV + 1]
        o2 = a2[:DIFF_DV] / a2[DIFF_DV:DIFF_DV + 1]
        oh = o1 - lam * o2
        ms = jnp.mean(oh * oh, axis=0, keepdims=True)
        heads.append(oh * lax.rsqrt(ms + EPS))
    ot = jnp.concatenate(heads, axis=0) * gain_ref[...] * (1.0 - lam_init)
    o_ref[...] = ot.T.astype(BF16)


def _diff(qt, k, vt, lq, lk, gain_t, lam_init):
    B, W, S = qt.shape
    T = T_ATT
    return pl.pallas_call(
        functools.partial(_diff_kernel, lam_init=lam_init),
        out_shape=jax.ShapeDtypeStruct((B, S, W), BF16),
        grid=(B, S // T),
        in_specs=[
            pl.BlockSpec((None, W, T), lambda b, i: (b, 0, i)),
            pl.BlockSpec((None, S, W), lambda b, i: (b, 0, 0)),
            pl.BlockSpec((None, S // T, W, T), lambda b, i: (b, 0, 0, 0)),
            _const_spec((2, DIFF_DH)),
            _const_spec((2, DIFF_DH)),
            _const_spec((W, T)),
        ],
        out_specs=pl.BlockSpec((None, T, W), lambda b, i: (b, i, 0)),
        compiler_params=_params("parallel", "arbitrary"),
        name="diff",
    )(qt, k, vt, lq, lk, gain_t)


J_S5 = LC_S5 // SUBLANES


def _cmul(ar, ai, br, bi):
    return ar * br - ai * bi, ar * bi + ai * br


def _s5_prep_kernel(ls_ref, lr_ref, li_ref, br_ref, bi_ref,
                    bbr_ref, bbi_ref, pw_ref, qw_ref):
    step = jnp.exp(ls_ref[...])
    lr, li = lr_ref[...], li_ref[...]
    mag = jnp.exp(lr * step)
    ab_re = mag * jnp.cos(li * step)
    ab_im = mag * jnp.sin(li * step)
    den = lr * lr + li * li
    nr, ni = ab_re - 1.0, ab_im
    cr = (nr * lr + ni * li) / den
    ci = (ni * lr - nr * li) / den
    bbr_ref[...] = cr * br_ref[...] - ci * bi_ref[...]
    bbi_ref[...] = cr * bi_ref[...] + ci * br_ref[...]

    def powers(ar, ai, n_rows, nbits):
        r1 = lax.broadcasted_iota(jnp.int32, (n_rows, S5_LANES), 0) + 1
        pr = jnp.ones((n_rows, S5_LANES), F32)
        pi = jnp.zeros((n_rows, S5_LANES), F32)
        for kbit in range(nbits):
            sel = ((r1 >> kbit) & 1) == 1
            fr = jnp.where(sel, ar, 1.0)
            fi = jnp.where(sel, ai, 0.0)
            pr, pi = _cmul(pr, pi, fr, fi)
            ar, ai = _cmul(ar, ai, ar, ai)
        return pr, pi

    nb = J_S5.bit_length()
    pr, pi = powers(ab_re, ab_im, J_S5, nb)
    pw_ref[0] = pr
    pw_ref[1] = pi
    ajr, aji = pr[J_S5 - 1:J_S5], pi[J_S5 - 1:J_S5]
    qr, qi = powers(ajr, aji, SUBLANES, 4)
    qw_ref[0] = qr
    qw_ref[1] = qi


def _s5_prep(ls_e, lr, li, br_e, bi_e):
    L = ls_e.shape[0]
    N = S5_LANES
    vec = pl.BlockSpec((None, 1, N), lambda l: (l, 0, 0))
    mat = pl.BlockSpec((None, S5_CH, N), lambda l: (l, 0, 0))
    return pl.pallas_call(
        _s5_prep_kernel,
        out_shape=(
            jax.ShapeDtypeStruct((L, S5_CH, N), F32),
            jax.ShapeDtypeStruct((L, S5_CH, N), F32),
            jax.ShapeDtypeStruct((L, 2, J_S5, N), F32),
            jax.ShapeDtypeStruct((L, 2, SUBLANES, N), F32),
        ),
        grid=(L,),
        in_specs=[vec, vec, vec, mat, mat],
        out_specs=(mat, mat,
                   pl.BlockSpec((None, 2, J_S5, N), lambda l: (l, 0, 0, 0)),
                   pl.BlockSpec((None, 2, SUBLANES, N), lambda l: (l, 0, 0, 0))),
        compiler_params=_params("arbitrary"),
        name="s5_prep",
    )(ls_e, lr, li, br_e, bi_e)


def _s5_kernel(u_ref, perm_ref, permt_ref, bb_ref, cm_ref, pw_ref, qw_ref, d_ref, wglu_ref, bglu_ref,
               o_ref, z_ref, xp_ref):
    N = S5_LANES
    J = J_S5

    @pl.when(pl.program_id(1) == 0)
    def _():
        xp_ref[...] = jnp.zeros_like(xp_ref)

    u = u_ref[...]
    u_p = _dot(perm_ref[...], u.astype(BF16)).astype(BF16)
    bu = _dot(u_p, bb_ref[...])

    ar, ai = pw_ref[0, 0:1, :], pw_ref[1, 0:1, :]
    zr = jnp.zeros((SUBLANES, N), F32)
    zi = jnp.zeros((SUBLANES, N), F32)
    for j in range(J):
        rs = slice(j * SUBLANES, (j + 1) * SUBLANES)
        zr, zi = (ar * zr - ai * zi + bu[rs, :N], ar * zi + ai * zr + bu[rs, N:])
        z_ref[rs, :N] = zr
        z_ref[rs, N:] = zi

    sub = lax.broadcasted_iota(jnp.int32, (SUBLANES, N), 0)
    er, ei = zr, zi
    for d in (1, 2, 4):
        mr, mi = qw_ref[0, d - 1:d, :], qw_ref[1, d - 1:d, :]
        sr = jnp.where(sub >= d, pltpu.roll(er, d, 0), 0.0)
        si = jnp.where(sub >= d, pltpu.roll(ei, d, 0), 0.0)
        er, ei = er + mr * sr - mi * si, ei + mr * si + mi * sr
    xpr, xpi = xp_ref[0:1, :], xp_ref[1:2, :]
    qr, qi = qw_ref[0], qw_ref[1]
    xer = er + qr * xpr - qi * xpi
    xei = ei + qr * xpi + qi * xpr
    xir = jnp.where(sub >= 1, pltpu.roll(xer, 1, 0), xpr)
    xii = jnp.where(sub >= 1, pltpu.roll(xei, 1, 0), xpi)
    xp_ref[0:1, :] = xer[SUBLANES - 1:SUBLANES]
    xp_ref[1:2, :] = xei[SUBLANES - 1:SUBLANES]

    for j in range(J):
        rs = slice(j * SUBLANES, (j + 1) * SUBLANES)
        pr, pi = pw_ref[0, j:j + 1, :], pw_ref[1, j:j + 1, :]
        z_ref[rs, :N] = z_ref[rs, :N] + pr * xir - pi * xii
        z_ref[rs, N:] = z_ref[rs, N:] + pr * xii + pi * xir

    y_p = _dot(z_ref[...].astype(BF16), cm_ref[...])
    y_hi, y_lo = _split2(y_p)
    y = _dot(permt_ref[...], y_hi) + _dot(permt_ref[...], y_lo)
    y = _gelu_tanh(y + d_ref[...] * u)
    gate = _sigmoid(_dot(y.astype(BF16), wglu_ref[...]) + bglu_ref[...])
    o_ref[...] = (y * gate).astype(BF16)


def _s5(u, perm, permt, bb, cm, pw, qw, d, wglu, bglu):
    B, S, W = u.shape
    Lc = LC_S5
    N = S5_LANES
    return pl.pallas_call(
        _s5_kernel,
        out_shape=jax.ShapeDtypeStruct((B, S, W), BF16),
        grid=(B, S // Lc),
        in_specs=[
            pl.BlockSpec((None, Lc, W), lambda b, t: (b, t, 0)),
            _const_spec((Lc, Lc)),
            _const_spec((Lc, Lc)),
            _const_spec((W, 2 * N)),
            _const_spec((2 * N, W)),
            _const_spec((2, J_S5, N)),
            _const_spec((2, SUBLANES, N)),
            _const_spec((1, W)),
            _const_spec((W, W)),
            _const_spec((1, W)),
        ],
        out_specs=pl.BlockSpec((None, Lc, W), lambda b, t: (b, t, 0)),
        scratch_shapes=[pltpu.VMEM((Lc, 2 * N), F32), pltpu.VMEM((2, N), F32)],
        compiler_params=_params("parallel", "arbitrary"),
        name="s5",
    )(u, perm, permt, bb, cm, pw, qw, d, wglu, bglu)


def _shift_rows(x, d, fill):
    n = x.shape[0]
    if d % SUBLANES == 0:
        return jnp.concatenate([jnp.full((d, x.shape[1]), fill, x.dtype), x[:n - d]], axis=0)
    r = lax.broadcasted_iota(jnp.int32, x.shape, 0)
    return jnp.where(r >= d, pltpu.roll(x, d, 0), fill)


def _lru_kernel(p_ref, cw_ref, cb_ref, wa_ref, ba_ref, wx_ref, bx_ref, lam_ref, o_ref, tail_ref, h_ref):
    Lt = p_ref.shape[0]
    W = GROUP_WIDTH

    @pl.when(pl.program_id(1) == 0)
    def _():
        tail_ref[...] = jnp.zeros_like(tail_ref)
        h_ref[...] = jnp.zeros_like(h_ref)

    xb = p_ref[:, :W]
    gate = p_ref[:, W:]
    ext = jnp.concatenate([tail_ref[...], xb], axis=0)
    tail_ref[...] = xb[Lt - SUBLANES:]
    xc = xb * cw_ref[LRU_CONV - 1:LRU_CONV, :] + cb_ref[...]
    for sh in range(1, LRU_CONV):
        xs = pltpu.roll(ext, sh, 0)[SUBLANES:]
        xc = xc + xs * cw_ref[LRU_CONV - 1 - sh:LRU_CONV - sh, :]
    xcb = xc.astype(BF16)
    r = _sigmoid(_dot(xcb, wa_ref[...]) + ba_ref[...])
    ig = _sigmoid(_dot(xcb, wx_ref[...]) + bx_ref[...])
    log_a = -LRU_C * r * _softplus(-lam_ref[...])
    a = jnp.exp(log_a)
    th = jnp.tanh(log_a)
    mult = jnp.sqrt(jnp.maximum(-2.0 * th / (1.0 - th), 1e-12))
    b = mult * (ig * xc)
    d = 1
    while d < Lt:
        b = b + a * _shift_rows(b, d, 0.0)
        a = a * _shift_rows(a, d, 1.0)
        d *= 2
    h = b + a * h_ref[...]
    h_ref[...] = h[Lt - 1:Lt]
    o_ref[...] = (h * _gelu_tanh(gate)).astype(BF16)


def _lru(p, cw, cb, wa, ba, wx, bx, lam):
    B, S, _ = p.shape
    Lt = LT_LRU
    W = GROUP_WIDTH
    return pl.pallas_call(
        _lru_kernel,
        out_shape=jax.ShapeDtypeStruct((B, S, W), BF16),
        grid=(B, S // Lt),
        in_specs=[
            pl.BlockSpec((None, Lt, 2 * W), lambda b, t: (b, t, 0)),
            _const_spec((LRU_CONV, W)),
            _const_spec((1, W)),
            _const_spec((W, W)),
            _const_spec((1, W)),
            _const_spec((W, W)),
            _const_spec((1, W)),
            _const_spec((1, W)),
        ],
        out_specs=pl.BlockSpec((None, Lt, W), lambda b, t: (b, t, 0)),
        scratch_shapes=[pltpu.VMEM((SUBLANES, W), F32), pltpu.VMEM((1, W), F32)],
        compiler_params=_params("parallel", "arbitrary"),
        name="lru",
    )(p, cw, cb, wa, ba, wx, bx, lam)


def _outffn_kernel(x_ref, oa_ref, ob_ref, oc_ref, od_ref, wo_ref, g1_ref, g2_ref, g3_ref,
                   w1_ref, w2_ref, out_ref):
    mixin = jnp.concatenate([oa_ref[...], ob_ref[...], oc_ref[...], od_ref[...]], axis=-1)
    x1 = x_ref[...] + _rms(_dot(mixin, wo_ref[...]), g1_ref[...])
    h = _rms(x1, g2_ref[...]).astype(BF16)
    f = jnp.zeros_like(x1)
    for c in range(D_FF // FF_CHUNK):
        a = jnp.maximum(_dot(h, w1_ref[:, c * FF_CHUNK:(c + 1) * FF_CHUNK]), 0.0)
        f = f + _dot((a * a).astype(BF16), w2_ref[c * FF_CHUNK:(c + 1) * FF_CHUNK, :])
    out_ref[...] = x1 + _rms(f, g3_ref[...])


def _outffn(x, oa, ob, oc, od, wo, g1, g2, g3, w1, w2):
    B, S, D = x.shape
    tm = TM_FFN
    W = GROUP_WIDTH
    row = lambda n: pl.BlockSpec((None, tm, n), lambda b, t: (b, t, 0))
    once = lambda shape: pl.BlockSpec(shape, lambda b, t: (0, 0), pipeline_mode=pl.Buffered(1))
    return pl.pallas_call(
        _outffn_kernel,
        out_shape=jax.ShapeDtypeStruct((B, S, D), F32),
        grid=(B, S // tm),
        in_specs=[row(D), row(W), row(W), row(W), row(W),
                  once((D, D)), _const_spec((1, D)), _const_spec((1, D)), _const_spec((1, D)),
                  once((D, D_FF)), once((D_FF, D))],
        out_specs=row(D),
        compiler_params=_params("parallel", "parallel"),
        name="outffn",
    )(x, oa, ob, oc, od, wo, g1, g2, g3, w1, w2)


def _rot_cols(w):
    d = w.shape[0]
    g = w.reshape(d, -1, 2, DIFF_DH // 2)
    return jnp.concatenate([-g[:, :, 1], g[:, :, 0]], axis=-1).reshape(d, -1)


def _rope_tables(S):
    inv = ROPE_THETA ** (-jnp.arange(0, DIFF_DH, 2, dtype=F32) / DIFF_DH)
    ang = jnp.arange(S, dtype=F32)[:, None] * inv[None, :]
    emb = jnp.concatenate([ang, ang], axis=-1)
    reps = GROUP_WIDTH // DIFF_DH
    return jnp.tile(jnp.cos(emb), (1, reps)), jnp.tile(jnp.sin(emb), (1, reps))


def _block_diag(blocks):
    n, a, b = blocks.shape
    eye = jnp.eye(n, dtype=blocks.dtype)
    return (eye[:, None, :, None] * blocks[:, :, None, :]).reshape(n * a, n * b)


def _s5_perm(Lc):
    r = np.arange(Lc)
    t = (r % SUBLANES) * (Lc // SUBLANES) + r // SUBLANES
    p = np.zeros((Lc, Lc), np.float32)
    p[r, t] = 1.0
    return p


def kernel(x, ln_mix_pre, ln_mix_post, ln_ffn_pre, ln_ffn_post, w_in, w_out, gla_w_gate, gla_b_gate, gla_norm, diff_lq1, diff_lk1, diff_lq2, diff_lk2, diff_norm, s5_log_step, s5_a_re, s5_a_im, s5_b_re, s5_b_im, s5_c_re, s5_c_im, s5_d, s5_w_glu, s5_b_glu, lru_conv_w, lru_conv_b, lru_w_a, lru_b_a, lru_w_x, lru_b_x, lru_lambda, ffn_w1, ffn_w2):
    B, S, D = x.shape
    L = w_in.shape[0]
    W = GROUP_WIDTH
    HK = GLA_HEADS * GLA_DK

    cos_n, sin_n = _rope_tables(S)
    cos_t = (cos_n * DIFF_DH ** -0.5).T
    sin_t = (sin_n * DIFF_DH ** -0.5).T
    tril = jnp.asarray(np.kron(np.eye(LT_GLA // GLA_CHUNK), np.tril(np.ones((GLA_CHUNK, GLA_CHUNK)))), BF16)
    gmean = jnp.asarray(np.kron(np.eye(GLA_HEADS), np.full((GLA_DV, GLA_DV), 1.0 / GLA_DV)), BF16)
    perm = _s5_perm(LC_S5)
    perm_b, permt_b = jnp.asarray(perm, BF16), jnp.asarray(perm.T, BF16)

    ls_e = jnp.repeat(s5_log_step, S5_STATE, axis=1)[:, None, :]
    lr_e = s5_a_re.reshape(L, 1, S5_LANES)
    li_e = s5_a_im.reshape(L, 1, S5_LANES)
    br_e = s5_b_re.transpose(0, 3, 1, 2).reshape(L, S5_CH, S5_LANES)
    bi_e = s5_b_im.transpose(0, 3, 1, 2).reshape(L, S5_CH, S5_LANES)
    bbr, bbi, pw, qw = _s5_prep(ls_e, lr_e, li_e, br_e, bi_e)
    grp = jnp.arange(S5_LANES) // S5_STATE
    gsel = (jnp.arange(S5_GROUPS)[:, None] == grp[None, :]).astype(F32)

    for l in range(L):
        lam_init = 0.8 - 0.6 * math.exp(-0.3 * l)
        wi = w_in[l]
        offs = np.cumsum([0, HK, HK, 2 * HK, GLA_RANK, 2 * HK, W, W, W, W, W, W])
        seg = [wi[:, offs[i]:offs[i + 1]] for i in range(11)]
        g_q, g_k, g_v, g_lr, g_og, d_q, d_k, d_v, s_u, r_x, r_g = seg
        glr_pad = jnp.pad(g_lr, ((0, 0), (0, LANES - GLA_RANK)))
        wn = jnp.concatenate([g_q, g_k, g_v, g_og, glr_pad, d_k, _rot_cols(d_k), s_u, r_x, r_g],
                             axis=1).astype(BF16)
        wt = jnp.concatenate([d_q, _rot_cols(d_q), d_v], axis=1).T.astype(BF16)

        gla_p, kd, qt, vt, su, rxg = _inproj(x, ln_mix_pre[l][None], wn, wt, cos_n, sin_n, cos_t, sin_t)

        wg = jnp.pad(gla_w_gate[l], ((0, LANES - GLA_RANK), (0, 0))).astype(BF16)
        o_a = _gla(gla_p, wg, gla_b_gate[l][None], tril, gmean,
                   jnp.tile(gla_norm[l], GLA_HEADS)[None])

        lq = jnp.stack([diff_lq1[l], diff_lq2[l]])
        lk = jnp.stack([diff_lk1[l], diff_lk2[l]])
        gain_t = jnp.broadcast_to(jnp.tile(diff_norm[l], DIFF_HEADS)[:, None], (W, T_ATT))
        o_b = _diff(qt, kd, vt, lq, lk, gain_t, lam_init)

        bb = jnp.concatenate(
            [(gsel[:, None, :] * b[None, :, :]).reshape(W, S5_LANES) for b in (bbr[l], bbi[l])],
            axis=1).astype(BF16)
        cm = jnp.concatenate([_block_diag(s5_c_re[l]).T, -_block_diag(s5_c_im[l]).T], axis=0).astype(BF16)
        o_c = _s5(su, perm_b, permt_b, bb, cm, pw[l], qw[l], s5_d[l][None],
                  s5_w_glu[l].astype(BF16), s5_b_glu[l][None])

        o_d = _lru(rxg, lru_conv_w[l], lru_conv_b[l][None],
                   _block_diag(lru_w_a[l]).astype(BF16), lru_b_a[l][None],
                   _block_diag(lru_w_x[l]).astype(BF16), lru_b_x[l][None],
                   lru_lambda[l][None])

        x = _outffn(x, o_a, o_b, o_c, o_d, w_out[l].astype(BF16),
                    ln_mix_post[l][None], ln_ffn_pre[l][None], ln_ffn_post[l][None],
                    ffn_w1[l].astype(BF16), ffn_w2[l].astype(BF16))
    return x
```

```python
import functools
import math

import numpy as np
import jax
import jax.numpy as jnp
from jax import lax
from jax.experimental import pallas as pl
from jax.experimental.pallas import tpu as pltpu

F32 = jnp.float32
BF16 = jnp.bfloat16

D_MODEL = 1024
GROUP_WIDTH = 256
EPS = 1e-6
GLA_HEADS, GLA_DK, GLA_DV, GLA_RANK, GLA_CHUNK = 4, 32, 64, 16, 64
GLA_GATE_NORM = 16.0
DIFF_HEADS, DIFF_DH, DIFF_DV = 4, 32, 64
ROPE_THETA = 10000.0
S5_GROUPS, S5_STATE, S5_CH = 16, 64, 16
S5_LANES = S5_GROUPS * S5_STATE
LRU_BLOCKS, LRU_BLOCK, LRU_CONV, LRU_C = 4, 64, 4, 8.0
D_FF = 4 * D_MODEL

SUBLANES = 8
LANES = 128
VMEM_LIMIT = 56 * 1024 * 1024

TM_INPROJ = 512
T_ATT = 256
LT_GLA = 256
LC_S5 = 256
LT_LRU = 256
TM_FFN = 512
FF_CHUNK = 1024


def _dot(a, b):
    return jnp.dot(a, b, preferred_element_type=F32)


def _dot_nt(a, b):
    return lax.dot_general(a, b, (((1,), (1,)), ((), ())), preferred_element_type=F32)


def _dot_tn(a, b):
    return lax.dot_general(a, b, (((0,), (0,)), ((), ())), preferred_element_type=F32)


def _split2(x):
    hi = x.astype(BF16)
    lo = (x - hi.astype(F32)).astype(BF16)
    return hi, lo


def _split3(x):
    hi = x.astype(BF16)
    r = x - hi.astype(F32)
    mid = r.astype(BF16)
    lo = (r - mid.astype(F32)).astype(BF16)
    return hi, mid, lo


def _sigmoid(x):
    return 1.0 / (1.0 + jnp.exp(-x))


def _softplus(x):
    return jnp.maximum(x, 0.0) + jnp.log1p(jnp.exp(-jnp.abs(x)))


def _gelu_tanh(x):
    c = math.sqrt(2.0 / math.pi)
    return 0.5 * x * (1.0 + jnp.tanh(c * (x + 0.044715 * (x * x * x))))


def _rms(x, g):
    ms = jnp.mean(x * x, axis=-1, keepdims=True)
    return x * lax.rsqrt(ms + EPS) * g


def _params(*sem):
    return pltpu.CompilerParams(dimension_semantics=sem, vmem_limit_bytes=VMEM_LIMIT)


def _const_spec(shape):
    nd = len(shape)
    return pl.BlockSpec(shape, lambda *_: (0,) * nd)


N_GLA = 896
N_NAT = N_GLA + 5 * GROUP_WIDTH
N_T = 3 * GROUP_WIDTH


def _inproj_kernel(x_ref, g_ref, wn_ref, wt_ref, cn_ref, sn_ref, ct_ref, st_ref,
                   gla_ref, kd_ref, qt_ref, vt_ref, su_ref, rxg_ref):
    h = _rms(x_ref[...], g_ref[...]).astype(BF16)
    W = GROUP_WIDTH
    gla_ref[...] = _dot(h, wn_ref[:, :N_GLA])
    o = N_GLA
    k = _dot(h, wn_ref[:, o:o + W])
    kr = _dot(h, wn_ref[:, o + W:o + 2 * W])
    kd_ref[...] = (k * cn_ref[...] + kr * sn_ref[...]).astype(BF16)
    su_ref[...] = _dot(h, wn_ref[:, o + 2 * W:o + 3 * W])
    rxg_ref[...] = _dot(h, wn_ref[:, o + 3 * W:o + 5 * W])
    q = _dot_nt(wt_ref[:W, :], h)
    qr = _dot_nt(wt_ref[W:2 * W, :], h)
    qt_ref[...] = (q * ct_ref[...] + qr * st_ref[...]).astype(BF16)
    vt = _dot_nt(wt_ref[2 * W:, :], h).astype(BF16)
    for j in range(vt_ref.shape[0]):
        vt_ref[j] = vt[:, j * T_ATT:(j + 1) * T_ATT]


def _inproj(x, g, wn, wt, cn, sn, ct, st):
    B, S, D = x.shape
    tm = TM_INPROJ
    W = GROUP_WIDTH
    nsub = tm // T_ATT
    out_shape = (
        jax.ShapeDtypeStruct((B, S, N_GLA), F32),
        jax.ShapeDtypeStruct((B, S, W), BF16),
        jax.ShapeDtypeStruct((B, W, S), BF16),
        jax.ShapeDtypeStruct((B, S // T_ATT, W, T_ATT), BF16),
        jax.ShapeDtypeStruct((B, S, W), F32),
        jax.ShapeDtypeStruct((B, S, 2 * W), F32),
    )
    row = lambda n: pl.BlockSpec((None, tm, n), lambda b, t: (b, t, 0))
    return pl.pallas_call(
        _inproj_kernel,
        out_shape=out_shape,
        grid=(B, S // tm),
        in_specs=[
            row(D),
            _const_spec((1, D)),
            _const_spec((D, N_NAT)),
            _const_spec((N_T, D)),
            pl.BlockSpec((tm, W), lambda b, t: (t, 0)),
            pl.BlockSpec((tm, W), lambda b, t: (t, 0)),
            pl.BlockSpec((W, tm), lambda b, t: (0, t)),
            pl.BlockSpec((W, tm), lambda b, t: (0, t)),
        ],
        out_specs=(
            row(N_GLA),
            row(W),
            pl.BlockSpec((None, W, tm), lambda b, t: (b, 0, t)),
            pl.BlockSpec((None, nsub, W, T_ATT), lambda b, t: (b, t, 0, 0)),
            row(W),
            row(2 * W),
        ),
        compiler_params=_params("parallel", "parallel"),
        name="inproj",
    )(x, g, wn, wt, cn, sn, ct, st)


def _gla_kernel(p_ref, wg_ref, bg_ref, tril_ref, gm_ref, gain_ref, o_ref, st_ref):
    Lt = p_ref.shape[0]
    C = GLA_CHUNK
    HK = GLA_HEADS * GLA_DK
    HV = GLA_HEADS * GLA_DV

    @pl.when(pl.program_id(1) == 0)
    def _():
        st_ref[...] = jnp.zeros_like(st_ref)

    q = p_ref[:, 0:HK] * (GLA_DK ** -0.5)
    k = p_ref[:, HK:2 * HK]
    v = p_ref[:, 2 * HK:2 * HK + HV].astype(BF16)
    og = p_ref[:, 2 * HK + HV:2 * HK + 2 * HV]
    glr = p_ref[:, 2 * HK + 2 * HV:].astype(BF16)

    g = -_softplus(-(_dot(glr, wg_ref[...]) + bg_ref[...])) * (1.0 / GLA_GATE_NORM)
    g_hi, g_lo = _split2(g)
    tril = tril_ref[...]
    bcum = _dot(tril, g_hi) + _dot(tril, g_lo)
    q_dec = (q * jnp.exp(bcum)).astype(BF16)
    k_dec = (k * jnp.exp(-bcum)).astype(BF16)

    lane_k = lax.broadcasted_iota(jnp.int32, (C, HK), 1) // GLA_DK
    r_a = lax.broadcasted_iota(jnp.int32, (GLA_HEADS * C, C), 0) % C
    c_a = lax.broadcasted_iota(jnp.int32, (GLA_HEADS * C, C), 1)
    causal = r_a >= c_a
    lane_v = lax.broadcasted_iota(jnp.int32, (C, HV), 1) // GLA_DV
    st_row = lax.broadcasted_iota(jnp.int32, (HV, HK), 0) // GLA_DV
    st_col = lax.broadcasted_iota(jnp.int32, (HV, HK), 1) // GLA_DK
    same_head = st_row == st_col

    outs = []
    for c in range(Lt // C):
        sl = slice(c * C, (c + 1) * C)
        qd, kd, vc = q_dec[sl], k_dec[sl], v[sl]
        bc = bcum[sl]
        blast = bc[C - 1:C, :]
        k_st = (k[sl] * jnp.exp(blast - bc)).astype(BF16)
        q_big = jnp.concatenate(
            [jnp.where(lane_k == h, qd, jnp.zeros_like(qd)) for h in range(GLA_HEADS)], axis=0)
        attn = jnp.where(causal, _dot_nt(q_big, kd), 0.0).astype(BF16)
        o_all = _dot(attn, vc)
        o_c = jnp.zeros((C, HV), F32)
        for h in range(GLA_HEADS):
            o_c = o_c + jnp.where(lane_v == h, o_all[h * C:(h + 1) * C], 0.0)
        state = st_ref[...]
        o_c = o_c + _dot_nt(qd, state.astype(BF16))
        kv_t = jnp.where(same_head, _dot_tn(vc, k_st), 0.0)
        st_ref[...] = state * jnp.exp(blast) + kv_t
        outs.append(o_c)
    o = jnp.concatenate(outs, axis=0)
    sq_hi, sq_lo = _split2(o * o)
    ms = _dot(sq_hi, gm_ref[...]) + _dot(sq_lo, gm_ref[...])
    o = o * lax.rsqrt(ms + EPS) * gain_ref[...]
    o_ref[...] = (o * (og * _sigmoid(og))).astype(BF16)


def _gla(p, wg, bg, tril, gm, gain):
    B, S, _ = p.shape
    Lt = LT_GLA
    HV = GLA_HEADS * GLA_DV
    HK = GLA_HEADS * GLA_DK
    return pl.pallas_call(
        _gla_kernel,
        out_shape=jax.ShapeDtypeStruct((B, S, HV), BF16),
        grid=(B, S // Lt),
        in_specs=[
            pl.BlockSpec((None, Lt, N_GLA), lambda b, t: (b, t, 0)),
            _const_spec((HK, HK)),
            _const_spec((1, HK)),
            _const_spec((Lt, Lt)),
            _const_spec((HV, HV)),
            _const_spec((1, HV)),
        ],
        out_specs=pl.BlockSpec((None, Lt, HV), lambda b, t: (b, t, 0)),
        scratch_shapes=[pltpu.VMEM((HV, HK), F32)],
        compiler_params=_params("parallel", "arbitrary"),
        name="gla",
    )(p, wg, bg, tril, gm, gain)


V_AUG = DIFF_DV + 16

def _diff_kernel(qt_ref, k_ref, vt_ref, lq_ref, lk_ref, gain_ref, o_ref, wq_ref, acc_ref, m_ref, s_ref,
                 *, lam_init):
    T = T_ATT
    NB = 2 * DIFF_HEADS
    i = pl.program_id(1)
    qt = qt_ref[...]
    row = lax.broadcasted_iota(jnp.int32, qt.shape, 0) // DIFF_DH
    for n in range(NB):
        wq_ref[:, n * T:(n + 1) * T] = jnp.where(row == n, qt, jnp.zeros_like(qt))
    acc_ref[...] = jnp.zeros_like(acc_ref)
    m_ref[...] = jnp.full(m_ref.shape, -1e30, F32)
    kk = lax.broadcasted_iota(jnp.int32, (T, T), 0)
    qq = lax.broadcasted_iota(jnp.int32, (T, T), 1)
    causal = kk <= qq
    ones = jnp.ones((V_AUG - DIFF_DV, T), BF16)

    def scores(j, h):
        kb = k_ref[pl.ds(pl.multiple_of(j * T, T), T), :]
        return _dot(kb, wq_ref[:, 2 * h * T:(2 * h + 2) * T])

    def block(j, j_next, cons, prod, masked):
        for h in range(DIFF_HEADS):
            s_ref[prod, h] = scores(j_next, h)
            va = jnp.concatenate([vt_ref[j, h * DIFF_DV:(h + 1) * DIFF_DV, :], ones], axis=0)
            for c in range(2):
                n = 2 * h + c
                sc = s_ref[cons, h, :, c * T:(c + 1) * T]
                if masked:
                    sc = jnp.where(causal, sc, -1e30)
                m = m_ref[n]
                m_new = jnp.maximum(m, jnp.max(sc, axis=0, keepdims=True))
                p = jnp.exp2(sc - m_new).astype(BF16)
                acc_ref[n] = jnp.exp2(m - m_new) * acc_ref[n] + _dot(va, p)
                m_ref[n] = m_new

    for h in range(DIFF_HEADS):
        s_ref[0, h] = scores(i, h)
    block(i, 0, 0, 1, True)

    def body(t, carry):
        block(2 * t, 2 * t + 1, 1, 0, False)
        block(2 * t + 1, jnp.minimum(2 * t + 2, i - 1), 0, 1, False)
        return carry

    lax.fori_loop(0, i >> 1, body, 0)

    @pl.when((i & 1) == 1)
    def _():
        block(i - 1, i - 1, 1, 0, False)

    e = jnp.exp(jnp.sum(lq_ref[...] * lk_ref[...], axis=-1, keepdims=True))
    lam = e[0:1] - e[1:2] + lam_init
    heads = []
    for h in range(DIFF_HEADS):
        a1, a2 = acc_ref[2 * h], acc_ref[2 * h + 1]
        o1 = a1[:DIFF_DV] / a1[DIFF_DV:DIFF_DV + 1]
        o2 = a2[:DIFF_DV] / a2[DIFF_DV:DIFF_DV + 1]
        oh = o1 - lam * o2
        ms = jnp.mean(oh * oh, axis=0, keepdims=True)
        heads.append(oh * lax.rsqrt(ms + EPS))
    ot = jnp.concatenate(heads, axis=0) * gain_ref[...] * (1.0 - lam_init)
    o_ref[...] = ot.T.astype(BF16)


def _diff(qt, k, vt, lq, lk, gain_t, lam_init):
    B, W, S = qt.shape
    T = T_ATT
    return pl.pallas_call(
        functools.partial(_diff_kernel, lam_init=lam_init),
        out_shape=jax.ShapeDtypeStruct((B, S, W), BF16),
        grid=(B, S // T),
        in_specs=[
            pl.BlockSpec((None, W, T), lambda b, i: (b, 0, i)),
            pl.BlockSpec((None, S, W), lambda b, i: (b, 0, 0)),
            pl.BlockSpec((None, S // T, W, T), lambda b, i: (b, 0, 0, 0)),
            _const_spec((2, DIFF_DH)),
            _const_spec((2, DIFF_DH)),
            _const_spec((W, T)),
        ],
        out_specs=pl.BlockSpec((None, T, W), lambda b, i: (b, i, 0)),
        scratch_shapes=[pltpu.VMEM((W, 2 * DIFF_HEADS * T), BF16),
                        pltpu.VMEM((2 * DIFF_HEADS, V_AUG, T), F32),
                        pltpu.VMEM((2 * DIFF_HEADS, 1, T), F32),
                        pltpu.VMEM((2, DIFF_HEADS, T, 2 * T), F32)],
        compiler_params=_params("parallel", "arbitrary"),
        name="diff",
    )(qt, k, vt, lq, lk, gain_t)


J_S5 = LC_S5 // SUBLANES


def _cmul(ar, ai, br, bi):
    return ar * br - ai * bi, ar * bi + ai * br


def _s5_prep_kernel(ls_ref, lr_ref, li_ref, br_ref, bi_ref,
                    bbr_ref, bbi_ref, pw_ref, qw_ref):
    step = jnp.exp(ls_ref[...])
    lr, li = lr_ref[...], li_ref[...]
    mag = jnp.exp(lr * step)
    ab_re = mag * jnp.cos(li * step)
    ab_im = mag * jnp.sin(li * step)
    den = lr * lr + li * li
    nr, ni = ab_re - 1.0, ab_im
    cr = (nr * lr + ni * li) / den
    ci = (ni * lr - nr * li) / den
    bbr_ref[...] = cr * br_ref[...] - ci * bi_ref[...]
    bbi_ref[...] = cr * bi_ref[...] + ci * br_ref[...]

    def powers(ar, ai, n_rows, nbits):
        r1 = lax.broadcasted_iota(jnp.int32, (n_rows, S5_LANES), 0) + 1
        pr = jnp.ones((n_rows, S5_LANES), F32)
        pi = jnp.zeros((n_rows, S5_LANES), F32)
        for kbit in range(nbits):
            sel = ((r1 >> kbit) & 1) == 1
            fr = jnp.where(sel, ar, 1.0)
            fi = jnp.where(sel, ai, 0.0)
            pr, pi = _cmul(pr, pi, fr, fi)
            ar, ai = _cmul(ar, ai, ar, ai)
        return pr, pi

    nb = J_S5.bit_length()
    pr, pi = powers(ab_re, ab_im, J_S5, nb)
    pw_ref[0] = pr
    pw_ref[1] = pi
    ajr, aji = pr[J_S5 - 1:J_S5], pi[J_S5 - 1:J_S5]
    qr, qi = powers(ajr, aji, SUBLANES, 4)
    qw_ref[0] = qr
    qw_ref[1] = qi


def _s5_prep(ls_e, lr, li, br_e, bi_e):
    L = ls_e.shape[0]
    N = S5_LANES
    vec = pl.BlockSpec((None, 1, N), lambda l: (l, 0, 0))
    mat = pl.BlockSpec((None, S5_CH, N), lambda l: (l, 0, 0))
    return pl.pallas_call(
        _s5_prep_kernel,
        out_shape=(
            jax.ShapeDtypeStruct((L, S5_CH, N), F32),
            jax.ShapeDtypeStruct((L, S5_CH, N), F32),
            jax.ShapeDtypeStruct((L, 2, J_S5, N), F32),
            jax.ShapeDtypeStruct((L, 2, SUBLANES, N), F32),
        ),
        grid=(L,),
        in_specs=[vec, vec, vec, mat, mat],
        out_specs=(mat, mat,
                   pl.BlockSpec((None, 2, J_S5, N), lambda l: (l, 0, 0, 0)),
                   pl.BlockSpec((None, 2, SUBLANES, N), lambda l: (l, 0, 0, 0))),
        compiler_params=_params("arbitrary"),
        name="s5_prep",
    )(ls_e, lr, li, br_e, bi_e)


def _s5_kernel(u_ref, perm_ref, permt_ref, bb_ref, cm_ref, pw_ref, qw_ref, d_ref, wglu_ref, bglu_ref,
               o_ref, z_ref, xp_ref):
    N = S5_LANES
    J = J_S5

    @pl.when(pl.program_id(1) == 0)
    def _():
        xp_ref[...] = jnp.zeros_like(xp_ref)

    u = u_ref[...]
    u_p = _dot(perm_ref[...], u.astype(BF16)).astype(BF16)
    bu = _dot(u_p, bb_ref[...])

    ar, ai = pw_ref[0, 0:1, :], pw_ref[1, 0:1, :]
    zr = jnp.zeros((SUBLANES, N), F32)
    zi = jnp.zeros((SUBLANES, N), F32)
    for j in range(J):
        rs = slice(j * SUBLANES, (j + 1) * SUBLANES)
        zr, zi = (ar * zr - ai * zi + bu[rs, :N], ar * zi + ai * zr + bu[rs, N:])
        z_ref[rs, :N] = zr
        z_ref[rs, N:] = zi

    sub = lax.broadcasted_iota(jnp.int32, (SUBLANES, N), 0)
    er, ei = zr, zi
    for d in (1, 2, 4):
        mr, mi = qw_ref[0, d - 1:d, :], qw_ref[1, d - 1:d, :]
        sr = jnp.where(sub >= d, pltpu.roll(er, d, 0), 0.0)
        si = jnp.where(sub >= d, pltpu.roll(ei, d, 0), 0.0)
        er, ei = er + mr * sr - mi * si, ei + mr * si + mi * sr
    xpr, xpi = xp_ref[0:1, :], xp_ref[1:2, :]
    qr, qi = qw_ref[0], qw_ref[1]
    xer = er + qr * xpr - qi * xpi
    xei = ei + qr * xpi + qi * xpr
    xir = jnp.where(sub >= 1, pltpu.roll(xer, 1, 0), xpr)
    xii = jnp.where(sub >= 1, pltpu.roll(xei, 1, 0), xpi)
    xp_ref[0:1, :] = xer[SUBLANES - 1:SUBLANES]
    xp_ref[1:2, :] = xei[SUBLANES - 1:SUBLANES]

    for j in range(J):
        rs = slice(j * SUBLANES, (j + 1) * SUBLANES)
        pr, pi = pw_ref[0, j:j + 1, :], pw_ref[1, j:j + 1, :]
        z_ref[rs, :N] = z_ref[rs, :N] + pr * xir - pi * xii
        z_ref[rs, N:] = z_ref[rs, N:] + pr * xii + pi * xir

    y_p = _dot(z_ref[...].astype(BF16), cm_ref[...])
    y_hi, y_lo = _split2(y_p)
    y = _dot(permt_ref[...], y_hi) + _dot(permt_ref[...], y_lo)
    y = _gelu_tanh(y + d_ref[...] * u)
    gate = _sigmoid(_dot(y.astype(BF16), wglu_ref[...]) + bglu_ref[...])
    o_ref[...] = (y * gate).astype(BF16)


def _s5(u, perm, permt, bb, cm, pw, qw, d, wglu, bglu):
    B, S, W = u.shape
    Lc = LC_S5
    N = S5_LANES
    return pl.pallas_call(
        _s5_kernel,
        out_shape=jax.ShapeDtypeStruct((B, S, W), BF16),
        grid=(B, S // Lc),
        in_specs=[
            pl.BlockSpec((None, Lc, W), lambda b, t: (b, t, 0)),
            _const_spec((Lc, Lc)),
            _const_spec((Lc, Lc)),
            _const_spec((W, 2 * N)),
            _const_spec((2 * N, W)),
            _const_spec((2, J_S5, N)),
            _const_spec((2, SUBLANES, N)),
            _const_spec((1, W)),
            _const_spec((W, W)),
            _const_spec((1, W)),
        ],
        out_specs=pl.BlockSpec((None, Lc, W), lambda b, t: (b, t, 0)),
        scratch_shapes=[pltpu.VMEM((Lc, 2 * N), F32), pltpu.VMEM((2, N), F32)],
        compiler_params=_params("parallel", "arbitrary"),
        name="s5",
    )(u, perm, permt, bb, cm, pw, qw, d, wglu, bglu)


def _shift_rows(x, d, fill):
    n = x.shape[0]
    if d % SUBLANES == 0:
        return jnp.concatenate([jnp.full((d, x.shape[1]), fill, x.dtype), x[:n - d]], axis=0)
    r = lax.broadcasted_iota(jnp.int32, x.shape, 0)
    return jnp.where(r >= d, pltpu.roll(x, d, 0), fill)


def _lru_kernel(p_ref, cw_ref, cb_ref, wa_ref, ba_ref, wx_ref, bx_ref, lam_ref, o_ref, tail_ref, h_ref):
    Lt = p_ref.shape[0]
    W = GROUP_WIDTH

    @pl.when(pl.program_id(1) == 0)
    def _():
        tail_ref[...] = jnp.zeros_like(tail_ref)
        h_ref[...] = jnp.zeros_like(h_ref)

    xb = p_ref[:, :W]
    gate = p_ref[:, W:]
    ext = jnp.concatenate([tail_ref[...], xb], axis=0)
    tail_ref[...] = xb[Lt - SUBLANES:]
    xc = xb * cw_ref[LRU_CONV - 1:LRU_CONV, :] + cb_ref[...]
    for sh in range(1, LRU_CONV):
        xs = pltpu.roll(ext, sh, 0)[SUBLANES:]
        xc = xc + xs * cw_ref[LRU_CONV - 1 - sh:LRU_CONV - sh, :]
    xcb = xc.astype(BF16)
    r = _sigmoid(_dot(xcb, wa_ref[...]) + ba_ref[...])
    ig = _sigmoid(_dot(xcb, wx_ref[...]) + bx_ref[...])
    log_a = -LRU_C * r * _softplus(-lam_ref[...])
    a = jnp.exp(log_a)
    th = jnp.tanh(log_a)
    mult = jnp.sqrt(jnp.maximum(-2.0 * th / (1.0 - th), 1e-12))
    b = mult * (ig * xc)
    d = 1
    while d < Lt:
        b = b + a * _shift_rows(b, d, 0.0)
        a = a * _shift_rows(a, d, 1.0)
        d *= 2
    h = b + a * h_ref[...]
    h_ref[...] = h[Lt - 1:Lt]
    o_ref[...] = (h * _gelu_tanh(gate)).astype(BF16)


def _lru(p, cw, cb, wa, ba, wx, bx, lam):
    B, S, _ = p.shape
    Lt = LT_LRU
    W = GROUP_WIDTH
    return pl.pallas_call(
        _lru_kernel,
        out_shape=jax.ShapeDtypeStruct((B, S, W), BF16),
        grid=(B, S // Lt),
        in_specs=[
            pl.BlockSpec((None, Lt, 2 * W), lambda b, t: (b, t, 0)),
            _const_spec((LRU_CONV, W)),
            _const_spec((1, W)),
            _const_spec((W, W)),
            _const_spec((1, W)),
            _const_spec((W, W)),
            _const_spec((1, W)),
            _const_spec((1, W)),
        ],
        out_specs=pl.BlockSpec((None, Lt, W), lambda b, t: (b, t, 0)),
        scratch_shapes=[pltpu.VMEM((SUBLANES, W), F32), pltpu.VMEM((1, W), F32)],
        compiler_params=_params("parallel", "arbitrary"),
        name="lru",
    )(p, cw, cb, wa, ba, wx, bx, lam)


def _outffn_kernel(x_ref, oa_ref, ob_ref, oc_ref, od_ref, wo_ref, g1_ref, g2_ref, g3_ref,
                   w1_ref, w2_ref, out_ref):
    mixin = jnp.concatenate([oa_ref[...], ob_ref[...], oc_ref[...], od_ref[...]], axis=-1)
    x1 = x_ref[...] + _rms(_dot(mixin, wo_ref[...]), g1_ref[...])
    h = _rms(x1, g2_ref[...]).astype(BF16)
    f = jnp.zeros_like(x1)
    for c in range(D_FF // FF_CHUNK):
        a = jnp.maximum(_dot(h, w1_ref[:, c * FF_CHUNK:(c + 1) * FF_CHUNK]), 0.0)
        f = f + _dot((a * a).astype(BF16), w2_ref[c * FF_CHUNK:(c + 1) * FF_CHUNK, :])
    out_ref[...] = x1 + _rms(f, g3_ref[...])


def _outffn(x, oa, ob, oc, od, wo, g1, g2, g3, w1, w2):
    B, S, D = x.shape
    tm = TM_FFN
    W = GROUP_WIDTH
    row = lambda n: pl.BlockSpec((None, tm, n), lambda b, t: (b, t, 0))
    once = lambda shape: pl.BlockSpec(shape, lambda b, t: (0, 0), pipeline_mode=pl.Buffered(1))
    return pl.pallas_call(
        _outffn_kernel,
        out_shape=jax.ShapeDtypeStruct((B, S, D), F32),
        grid=(B, S // tm),
        in_specs=[row(D), row(W), row(W), row(W), row(W),
                  once((D, D)), _const_spec((1, D)), _const_spec((1, D)), _const_spec((1, D)),
                  once((D, D_FF)), once((D_FF, D))],
        out_specs=row(D),
        compiler_params=_params("parallel", "parallel"),
        name="outffn",
    )(x, oa, ob, oc, od, wo, g1, g2, g3, w1, w2)


def _rot_cols(w):
    d = w.shape[0]
    g = w.reshape(d, -1, 2, DIFF_DH // 2)
    return jnp.concatenate([-g[:, :, 1], g[:, :, 0]], axis=-1).reshape(d, -1)


def _rope_tables(S):
    inv = ROPE_THETA ** (-jnp.arange(0, DIFF_DH, 2, dtype=F32) / DIFF_DH)
    ang = jnp.arange(S, dtype=F32)[:, None] * inv[None, :]
    emb = jnp.concatenate([ang, ang], axis=-1)
    reps = GROUP_WIDTH // DIFF_DH
    return jnp.tile(jnp.cos(emb), (1, reps)), jnp.tile(jnp.sin(emb), (1, reps))


def _block_diag(blocks):
    n, a, b = blocks.shape
    eye = jnp.eye(n, dtype=blocks.dtype)
    return (eye[:, None, :, None] * blocks[:, :, None, :]).reshape(n * a, n * b)


def _s5_perm(Lc):
    r = np.arange(Lc)
    t = (r % SUBLANES) * (Lc // SUBLANES) + r // SUBLANES
    p = np.zeros((Lc, Lc), np.float32)
    p[r, t] = 1.0
    return p


def kernel(x, ln_mix_pre, ln_mix_post, ln_ffn_pre, ln_ffn_post, w_in, w_out, gla_w_gate, gla_b_gate, gla_norm, diff_lq1, diff_lk1, diff_lq2, diff_lk2, diff_norm, s5_log_step, s5_a_re, s5_a_im, s5_b_re, s5_b_im, s5_c_re, s5_c_im, s5_d, s5_w_glu, s5_b_glu, lru_conv_w, lru_conv_b, lru_w_a, lru_b_a, lru_w_x, lru_b_x, lru_lambda, ffn_w1, ffn_w2):
    B, S, D = x.shape
    L = w_in.shape[0]
    W = GROUP_WIDTH
    HK = GLA_HEADS * GLA_DK

    cos_n, sin_n = _rope_tables(S)
    q_scale = DIFF_DH ** -0.5 * math.log2(math.e)
    cos_t = (cos_n * q_scale).T
    sin_t = (sin_n * q_scale).T
    tril = jnp.asarray(np.kron(np.eye(LT_GLA // GLA_CHUNK), np.tril(np.ones((GLA_CHUNK, GLA_CHUNK)))), BF16)
    gmean = jnp.asarray(np.kron(np.eye(GLA_HEADS), np.full((GLA_DV, GLA_DV), 1.0 / GLA_DV)), BF16)
    perm = _s5_perm(LC_S5)
    perm_b, permt_b = jnp.asarray(perm, BF16), jnp.asarray(perm.T, BF16)

    ls_e = jnp.repeat(s5_log_step, S5_STATE, axis=1)[:, None, :]
    lr_e = s5_a_re.reshape(L, 1, S5_LANES)
    li_e = s5_a_im.reshape(L, 1, S5_LANES)
    br_e = s5_b_re.transpose(0, 3, 1, 2).reshape(L, S5_CH, S5_LANES)
    bi_e = s5_b_im.transpose(0, 3, 1, 2).reshape(L, S5_CH, S5_LANES)
    bbr, bbi, pw, qw = _s5_prep(ls_e, lr_e, li_e, br_e, bi_e)
    grp = jnp.arange(S5_LANES) // S5_STATE
    gsel = (jnp.arange(S5_GROUPS)[:, None] == grp[None, :]).astype(F32)

    for l in range(L):
        lam_init = 0.8 - 0.6 * math.exp(-0.3 * l)
        wi = w_in[l]
        offs = np.cumsum([0, HK, HK, 2 * HK, GLA_RANK, 2 * HK, W, W, W, W, W, W])
        seg = [wi[:, offs[i]:offs[i + 1]] for i in range(11)]
        g_q, g_k, g_v, g_lr, g_og, d_q, d_k, d_v, s_u, r_x, r_g = seg
        glr_pad = jnp.pad(g_lr, ((0, 0), (0, LANES - GLA_RANK)))
        wn = jnp.concatenate([g_q, g_k, g_v, g_og, glr_pad, d_k, _rot_cols(d_k), s_u, r_x, r_g],
                             axis=1).astype(BF16)
        wt = jnp.concatenate([d_q, _rot_cols(d_q), d_v], axis=1).T.astype(BF16)

        gla_p, kd, qt, vt, su, rxg = _inproj(x, ln_mix_pre[l][None], wn, wt, cos_n, sin_n, cos_t, sin_t)

        wg = jnp.pad(gla_w_gate[l], ((0, LANES - GLA_RANK), (0, 0))).astype(BF16)
        o_a = _gla(gla_p, wg, gla_b_gate[l][None], tril, gmean,
                   jnp.tile(gla_norm[l], GLA_HEADS)[None])

        lq = jnp.stack([diff_lq1[l], diff_lq2[l]])
        lk = jnp.stack([diff_lk1[l], diff_lk2[l]])
        gain_t = jnp.broadcast_to(jnp.tile(diff_norm[l], DIFF_HEADS)[:, None], (W, T_ATT))
        o_b = _diff(qt, kd, vt, lq, lk, gain_t, lam_init)

        bb = jnp.concatenate(
            [(gsel[:, None, :] * b[None, :, :]).reshape(W, S5_LANES) for b in (bbr[l], bbi[l])],
            axis=1).astype(BF16)
        cm = jnp.concatenate([_block_diag(s5_c_re[l]).T, -_block_diag(s5_c_im[l]).T], axis=0).astype(BF16)
        o_c = _s5(su, perm_b, permt_b, bb, cm, pw[l], qw[l], s5_d[l][None],
                  s5_w_glu[l].astype(BF16), s5_b_glu[l][None])

        o_d = _lru(rxg, lru_conv_w[l], lru_conv_b[l][None],
                   _block_diag(lru_w_a[l]).astype(BF16), lru_b_a[l][None],
                   _block_diag(lru_w_x[l]).astype(BF16), lru_b_x[l][None],
                   lru_lambda[l][None])

        x = _outffn(x, o_a, o_b, o_c, o_d, w_out[l].astype(BF16),
                    ln_mix_post[l][None], ln_ffn_pre[l][None], ln_ffn_post[l][None],
                    ffn_w1[l].astype(BF16), ffn_w2[l].astype(BF16))
    return x
```

```python
import functools
import math

import numpy as np
import jax
import jax.numpy as jnp
from jax import lax
from jax.experimental import pallas as pl
from jax.experimental.pallas import tpu as pltpu

F32 = jnp.float32
BF16 = jnp.bfloat16

D_MODEL = 1024
GROUP_WIDTH = 256
EPS = 1e-6
GLA_HEADS, GLA_DK, GLA_DV, GLA_RANK, GLA_CHUNK = 4, 32, 64, 16, 64
GLA_GATE_NORM = 16.0
DIFF_HEADS, DIFF_DH, DIFF_DV = 4, 32, 64
ROPE_THETA = 10000.0
S5_GROUPS, S5_STATE, S5_CH = 16, 64, 16
S5_LANES = S5_GROUPS * S5_STATE
LRU_BLOCKS, LRU_BLOCK, LRU_CONV, LRU_C = 4, 64, 4, 8.0
D_FF = 4 * D_MODEL

SUBLANES = 8
LANES = 128
VMEM_LIMIT = 56 * 1024 * 1024

TM_INPROJ = 512
T_ATT = 256
LT_GLA = 256
TS_S5 = 32
NSUB_S5 = 4
LT_LRU = 256
TM_FFN = 512
FF_CHUNK = 1024


def _dot(a, b):
    return jnp.dot(a, b, preferred_element_type=F32)


def _dot_nt(a, b):
    return lax.dot_general(a, b, (((1,), (1,)), ((), ())), preferred_element_type=F32)


def _dot_tn(a, b):
    return lax.dot_general(a, b, (((0,), (0,)), ((), ())), preferred_element_type=F32)


def _split2(x):
    hi = x.astype(BF16)
    lo = (x - hi.astype(F32)).astype(BF16)
    return hi, lo


def _split3(x):
    hi = x.astype(BF16)
    r = x - hi.astype(F32)
    mid = r.astype(BF16)
    lo = (r - mid.astype(F32)).astype(BF16)
    return hi, mid, lo


def _sigmoid(x):
    return 1.0 / (1.0 + jnp.exp(-x))


def _softplus(x):
    return jnp.maximum(x, 0.0) + jnp.log1p(jnp.exp(-jnp.abs(x)))


def _gelu_tanh(x):
    c = math.sqrt(2.0 / math.pi)
    return 0.5 * x * (1.0 + jnp.tanh(c * (x + 0.044715 * (x * x * x))))


def _rms(x, g):
    ms = jnp.mean(x * x, axis=-1, keepdims=True)
    return x * lax.rsqrt(ms + EPS) * g


def _params(*sem):
    return pltpu.CompilerParams(dimension_semantics=sem, vmem_limit_bytes=VMEM_LIMIT)


def _const_spec(shape):
    nd = len(shape)
    return pl.BlockSpec(shape, lambda *_: (0,) * nd)


N_GLA = 896
N_NAT = N_GLA + 4 * GROUP_WIDTH
N_T = 2 * GROUP_WIDTH


def _inproj_kernel(x_ref, g_ref, wn_ref, wt_ref, cn_ref, sn_ref, ct_ref, st_ref,
                   gla_ref, kd_ref, qt_ref, vt_ref, su_ref, rxg_ref):
    h = _rms(x_ref[...], g_ref[...]).astype(BF16)
    W = GROUP_WIDTH
    HALF = DIFF_DH // 2
    gla_ref[...] = _dot(h, wn_ref[:, :N_GLA])
    o = N_GLA
    k = _dot(h, wn_ref[:, o:o + W])
    first_half = lax.broadcasted_iota(jnp.int32, k.shape, 1) % DIFF_DH < HALF
    k_sw = jnp.where(first_half, pltpu.roll(k, W - HALF, 1), pltpu.roll(k, HALF, 1))
    kd_ref[...] = (k * cn_ref[...] + k_sw * sn_ref[...]).astype(BF16)
    su_ref[...] = _dot(h, wn_ref[:, o + W:o + 2 * W])
    rxg_ref[...] = _dot(h, wn_ref[:, o + 2 * W:o + 4 * W])
    q = _dot_nt(wt_ref[:W, :], h)
    q_sw = jnp.concatenate(
        [q[g * DIFF_DH + off:g * DIFF_DH + off + HALF] for g in range(W // DIFF_DH) for off in (HALF, 0)],
        axis=0)
    qt_ref[...] = (q * ct_ref[...] + q_sw * st_ref[...]).astype(BF16)
    vt = _dot_nt(wt_ref[W:, :], h).astype(BF16)
    for j in range(vt_ref.shape[0]):
        vt_ref[j] = vt[:, j * T_ATT:(j + 1) * T_ATT]


def _inproj(x, g, wn, wt, cn, sn, ct, st):
    B, S, D = x.shape
    tm = TM_INPROJ
    W = GROUP_WIDTH
    nsub = tm // T_ATT
    out_shape = (
        jax.ShapeDtypeStruct((B, S, N_GLA), F32),
        jax.ShapeDtypeStruct((B, S, W), BF16),
        jax.ShapeDtypeStruct((B, W, S), BF16),
        jax.ShapeDtypeStruct((B, S // T_ATT, W, T_ATT), BF16),
        jax.ShapeDtypeStruct((S, B * W), F32),
        jax.ShapeDtypeStruct((B, S, 2 * W), F32),
    )
    row = lambda n: pl.BlockSpec((None, tm, n), lambda b, t: (b, t, 0))
    return pl.pallas_call(
        _inproj_kernel,
        out_shape=out_shape,
        grid=(B, S // tm),
        in_specs=[
            row(D),
            _const_spec((1, D)),
            _const_spec((D, N_NAT)),
            _const_spec((N_T, D)),
            pl.BlockSpec((tm, W), lambda b, t: (t, 0)),
            pl.BlockSpec((tm, W), lambda b, t: (t, 0)),
            pl.BlockSpec((W, tm), lambda b, t: (0, t)),
            pl.BlockSpec((W, tm), lambda b, t: (0, t)),
        ],
        out_specs=(
            row(N_GLA),
            row(W),
            pl.BlockSpec((None, W, tm), lambda b, t: (b, 0, t)),
            pl.BlockSpec((None, nsub, W, T_ATT), lambda b, t: (b, t, 0, 0)),
            pl.BlockSpec((tm, W), lambda b, t: (t, b)),
            row(2 * W),
        ),
        compiler_params=_params("parallel", "parallel"),
        name="inproj",
    )(x, g, wn, wt, cn, sn, ct, st)


def _gla_kernel(p_ref, wg_ref, bg_ref, tril_ref, gm_ref, gain_ref, o_ref, st_ref):
    Lt = p_ref.shape[0]
    C = GLA_CHUNK
    HK = GLA_HEADS * GLA_DK
    HV = GLA_HEADS * GLA_DV

    @pl.when(pl.program_id(1) == 0)
    def _():
        st_ref[...] = jnp.zeros_like(st_ref)

    q = p_ref[:, 0:HK] * (GLA_DK ** -0.5)
    k = p_ref[:, HK:2 * HK]
    v = p_ref[:, 2 * HK:2 * HK + HV].astype(BF16)
    og = p_ref[:, 2 * HK + HV:2 * HK + 2 * HV]
    glr = p_ref[:, 2 * HK + 2 * HV:].astype(BF16)

    g = -_softplus(-(_dot(glr, wg_ref[...]) + bg_ref[...])) * (1.0 / GLA_GATE_NORM)
    g_hi, g_lo = _split2(g)
    tril = tril_ref[...]
    bcum = _dot(tril, g_hi) + _dot(tril, g_lo)
    q_dec = (q * jnp.exp(bcum)).astype(BF16)
    k_dec = (k * jnp.exp(-bcum)).astype(BF16)

    lane_k = lax.broadcasted_iota(jnp.int32, (C, HK), 1) // GLA_DK
    r_a = lax.broadcasted_iota(jnp.int32, (GLA_HEADS * C, C), 0) % C
    c_a = lax.broadcasted_iota(jnp.int32, (GLA_HEADS * C, C), 1)
    causal = r_a >= c_a
    lane_v = lax.broadcasted_iota(jnp.int32, (C, HV), 1) // GLA_DV
    st_row = lax.broadcasted_iota(jnp.int32, (HV, HK), 0) // GLA_DV
    st_col = lax.broadcasted_iota(jnp.int32, (HV, HK), 1) // GLA_DK
    same_head = st_row == st_col

    outs = []
    for c in range(Lt // C):
        sl = slice(c * C, (c + 1) * C)
        qd, kd, vc = q_dec[sl], k_dec[sl], v[sl]
        bc = bcum[sl]
        blast = bc[C - 1:C, :]
        k_st = (k[sl] * jnp.exp(blast - bc)).astype(BF16)
        q_big = jnp.concatenate(
            [jnp.where(lane_k == h, qd, jnp.zeros_like(qd)) for h in range(GLA_HEADS)], axis=0)
        attn = jnp.where(causal, _dot_nt(q_big, kd), 0.0).astype(BF16)
        o_all = _dot(attn, vc)
        o_c = jnp.zeros((C, HV), F32)
        for h in range(GLA_HEADS):
            o_c = o_c + jnp.where(lane_v == h, o_all[h * C:(h + 1) * C], 0.0)
        state = st_ref[...]
        o_c = o_c + _dot_nt(qd, state.astype(BF16))
        kv_t = jnp.where(same_head, _dot_tn(vc, k_st), 0.0)
        st_ref[...] = state * jnp.exp(blast) + kv_t
        outs.append(o_c)
    o = jnp.concatenate(outs, axis=0)
    sq_hi, sq_lo = _split2(o * o)
    ms = _dot(sq_hi, gm_ref[...]) + _dot(sq_lo, gm_ref[...])
    o = o * lax.rsqrt(ms + EPS) * gain_ref[...]
    o_ref[...] = (o * (og * _sigmoid(og))).astype(BF16)


def _gla(p, wg, bg, tril, gm, gain):
    B, S, _ = p.shape
    Lt = LT_GLA
    HV = GLA_HEADS * GLA_DV
    HK = GLA_HEADS * GLA_DK
    return pl.pallas_call(
        _gla_kernel,
        out_shape=jax.ShapeDtypeStruct((B, S, HV), BF16),
        grid=(B, S // Lt),
        in_specs=[
            pl.BlockSpec((None, Lt, N_GLA), lambda b, t: (b, t, 0)),
            _const_spec((HK, HK)),
            _const_spec((1, HK)),
            _const_spec((Lt, Lt)),
            _const_spec((HV, HV)),
            _const_spec((1, HV)),
        ],
        out_specs=pl.BlockSpec((None, Lt, HV), lambda b, t: (b, t, 0)),
        scratch_shapes=[pltpu.VMEM((HV, HK), F32)],
        compiler_params=_params("parallel", "arbitrary"),
        name="gla",
    )(p, wg, bg, tril, gm, gain)


V_AUG = DIFF_DV + 16

def _diff_kernel(qt_ref, k_ref, vt_ref, lq_ref, lk_ref, gain_ref, o_ref, wq_ref, acc_ref, m_ref, s_ref,
                 mx_ref, *, lam_init):
    T = T_ATT
    NB = 2 * DIFF_HEADS
    i = pl.program_id(1)
    qt = qt_ref[...]
    row = lax.broadcasted_iota(jnp.int32, qt.shape, 0) // DIFF_DH
    for n in range(NB):
        wq_ref[:, n * T:(n + 1) * T] = jnp.where(row == n, qt, jnp.zeros_like(qt))
    acc_ref[...] = jnp.zeros_like(acc_ref)
    m_ref[...] = jnp.full(m_ref.shape, -1e30, F32)
    ones = jnp.ones((V_AUG - DIFF_DV, T), BF16)

    def produce(j, h, buf, diagonal):
        kb = k_ref[pl.ds(pl.multiple_of(j * T, T), T), :]
        s = _dot(kb, wq_ref[:, 2 * h * T:(2 * h + 2) * T])
        if diagonal:
            kk = lax.broadcasted_iota(jnp.int32, s.shape, 0)
            qq = lax.broadcasted_iota(jnp.int32, s.shape, 1)
            s = jnp.where(kk <= jnp.where(qq >= T, qq - T, qq), s, -1e30)
        s_ref[buf, h] = s
        mx_ref[buf, h] = jnp.max(s, axis=0, keepdims=True)

    def block(j, j_next, cons, prod):
        for h in range(DIFF_HEADS):
            produce(j_next, h, prod, False)
            va = jnp.concatenate([vt_ref[j, h * DIFF_DV:(h + 1) * DIFF_DV, :], ones], axis=0)
            for c in range(2):
                n = 2 * h + c
                m = m_ref[n]
                m_new = jnp.maximum(m, mx_ref[cons, h, :, c * T:(c + 1) * T])
                p = jnp.exp2(s_ref[cons, h, :, c * T:(c + 1) * T] - m_new).astype(BF16)
                acc_ref[n] = jnp.exp2(m - m_new) * acc_ref[n] + _dot(va, p)
                m_ref[n] = m_new

    for h in range(DIFF_HEADS):
        produce(i, h, 0, True)
    block(i, 0, 0, 1)

    def body(t, carry):
        block(2 * t, 2 * t + 1, 1, 0)
        block(2 * t + 1, jnp.minimum(2 * t + 2, i - 1), 0, 1)
        return carry

    lax.fori_loop(0, i >> 1, body, 0)

    @pl.when((i & 1) == 1)
    def _():
        block(i - 1, i - 1, 1, 0)

    e = jnp.exp(jnp.sum(lq_ref[...] * lk_ref[...], axis=-1, keepdims=True))
    lam = e[0:1] - e[1:2] + lam_init
    heads = []
    for h in range(DIFF_HEADS):
        a1, a2 = acc_ref[2 * h], acc_ref[2 * h + 1]
        o1 = a1[:DIFF_DV] / a1[DIFF_DV:DIFF_DV + 1]
        o2 = a2[:DIFF_DV] / a2[DIFF_DV:DIFF_DV + 1]
        oh = o1 - lam * o2
        ms = jnp.mean(oh * oh, axis=0, keepdims=True)
        heads.append(oh * lax.rsqrt(ms + EPS))
    ot = jnp.concatenate(heads, axis=0) * gain_ref[...] * (1.0 - lam_init)
    o_ref[...] = ot.T.astype(BF16)


def _diff(qt, k, vt, lq, lk, gain_t, lam_init):
    B, W, S = qt.shape
    T = T_ATT
    return pl.pallas_call(
        functools.partial(_diff_kernel, lam_init=lam_init),
        out_shape=jax.ShapeDtypeStruct((B, S, W), BF16),
        grid=(B, S // T),
        in_specs=[
            pl.BlockSpec((None, W, T), lambda b, i: (b, 0, i)),
            pl.BlockSpec((None, S, W), lambda b, i: (b, 0, 0)),
            pl.BlockSpec((None, S // T, W, T), lambda b, i: (b, 0, 0, 0)),
            _const_spec((2, DIFF_DH)),
            _const_spec((2, DIFF_DH)),
            _const_spec((W, T)),
        ],
        out_specs=pl.BlockSpec((None, T, W), lambda b, i: (b, i, 0)),
        scratch_shapes=[pltpu.VMEM((W, 2 * DIFF_HEADS * T), BF16),
                        pltpu.VMEM((2 * DIFF_HEADS, V_AUG, T), F32),
                        pltpu.VMEM((2 * DIFF_HEADS, 1, T), F32),
                        pltpu.VMEM((2, DIFF_HEADS, T, 2 * T), F32),
                        pltpu.VMEM((2, DIFF_HEADS, 1, 2 * T), F32)],
        compiler_params=_params("parallel", "arbitrary"),
        name="diff",
    )(qt, k, vt, lq, lk, gain_t)


def _s5_prep_kernel(ls_ref, lr_ref, li_ref, br_ref, bi_ref,
                    bbr_ref, bbi_ref, ab_ref):
    step = jnp.exp(ls_ref[...])
    lr, li = lr_ref[...], li_ref[...]
    mag = jnp.exp(lr * step)
    ab_re = mag * jnp.cos(li * step)
    ab_im = mag * jnp.sin(li * step)
    den = lr * lr + li * li
    nr, ni = ab_re - 1.0, ab_im
    cr = (nr * lr + ni * li) / den
    ci = (ni * lr - nr * li) / den
    bbr_ref[...] = cr * br_ref[...] - ci * bi_ref[...]
    bbi_ref[...] = cr * bi_ref[...] + ci * br_ref[...]

    ab_ref[0] = jnp.broadcast_to(ab_re, (SUBLANES, S5_LANES))
    ab_ref[1] = jnp.broadcast_to(ab_im, (SUBLANES, S5_LANES))


def _s5_prep(ls_e, lr, li, br_e, bi_e):
    L = ls_e.shape[0]
    N = S5_LANES
    vec = pl.BlockSpec((None, 1, N), lambda l: (l, 0, 0))
    mat = pl.BlockSpec((None, S5_CH, N), lambda l: (l, 0, 0))
    return pl.pallas_call(
        _s5_prep_kernel,
        out_shape=(
            jax.ShapeDtypeStruct((L, S5_CH, N), F32),
            jax.ShapeDtypeStruct((L, S5_CH, N), F32),
            jax.ShapeDtypeStruct((L, 2, SUBLANES, N), F32),
        ),
        grid=(L,),
        in_specs=[vec, vec, vec, mat, mat],
        out_specs=(mat, mat, pl.BlockSpec((None, 2, SUBLANES, N), lambda l: (l, 0, 0, 0))),
        compiler_params=_params("arbitrary"),
        name="s5_prep",
    )(ls_e, lr, li, br_e, bi_e)


def _s5_kernel(u_ref, perm_ref, permt_ref, bb_ref, cm_ref, ab_ref, d_ref, wglu_ref, bglu_ref,
               o_ref, z_ref, st_ref):
    N = S5_LANES
    W = GROUP_WIDTH
    TS = TS_S5
    nsub = u_ref.shape[0] // TS

    @pl.when(pl.program_id(0) == 0)
    def _():
        st_ref[...] = jnp.zeros_like(st_ref)

    def slab(k):
        return jnp.concatenate(
            [u_ref[k * TS:(k + 1) * TS, b * W:(b + 1) * W] for b in range(SUBLANES)], axis=0)

    def project(k):
        u_tb = _dot(perm_ref[...], slab(k).astype(BF16)).astype(BF16)
        z_ref[k] = _dot(u_tb, bb_ref[...])

    ar, ai = ab_ref[0], ab_ref[1]
    project(0)
    if nsub > 1:
        project(1)
    for k in range(nsub):
        zk = z_ref.at[k]
        zr, zi = st_ref[0], st_ref[1]
        for t in range(TS):
            rs = slice(t * SUBLANES, (t + 1) * SUBLANES)
            zr, zi = (ar * zr - ai * zi + zk[rs, :N], ar * zi + ai * zr + zk[rs, N:])
            zk[rs, :N] = zr
            zk[rs, N:] = zi
        st_ref[0] = zr
        st_ref[1] = zi

        half = SUBLANES * TS // 2
        y_tb = jnp.concatenate([_dot(zk[:half, :].astype(BF16), cm_ref[...]),
                                _dot(zk[half:, :].astype(BF16), cm_ref[...])], axis=0)
        if k + 2 < nsub:
            project(k + 2)
        y_hi, y_lo = _split2(y_tb)
        y = _dot(permt_ref[...], y_hi) + _dot(permt_ref[...], y_lo)
        y = _gelu_tanh(y + d_ref[...] * slab(k))
        gate = _sigmoid(_dot(y.astype(BF16), wglu_ref[...]) + bglu_ref[...])
        out = (y * gate).astype(BF16)
        for b in range(SUBLANES):
            o_ref[k * TS:(k + 1) * TS, b * W:(b + 1) * W] = out[b * TS:(b + 1) * TS]


def _s5(u, perm, permt, bb, cm, ab, d, wglu, bglu):
    S, BW = u.shape
    W = GROUP_WIDTH
    assert BW == SUBLANES * W, "the S5 kernel maps the batch onto the 8 sublanes"
    rows = SUBLANES * TS_S5
    Lt = TS_S5 * NSUB_S5
    N = S5_LANES
    return pl.pallas_call(
        _s5_kernel,
        out_shape=jax.ShapeDtypeStruct((S, BW), BF16),
        grid=(S // Lt,),
        in_specs=[
            pl.BlockSpec((Lt, BW), lambda t: (t, 0)),
            _const_spec((rows, rows)),
            _const_spec((rows, rows)),
            _const_spec((W, 2 * N)),
            _const_spec((2 * N, W)),
            _const_spec((2, SUBLANES, N)),
            _const_spec((1, W)),
            _const_spec((W, W)),
            _const_spec((1, W)),
        ],
        out_specs=pl.BlockSpec((Lt, BW), lambda t: (t, 0)),
        scratch_shapes=[pltpu.VMEM((NSUB_S5, rows, 2 * N), F32), pltpu.VMEM((2, SUBLANES, N), F32)],
        compiler_params=_params("arbitrary"),
        name="s5",
    )(u, perm, permt, bb, cm, ab, d, wglu, bglu)


def _shift_rows(x, d, fill):
    n = x.shape[0]
    if d % SUBLANES == 0:
        return jnp.concatenate([jnp.full((d, x.shape[1]), fill, x.dtype), x[:n - d]], axis=0)
    r = lax.broadcasted_iota(jnp.int32, x.shape, 0)
    return jnp.where(r >= d, pltpu.roll(x, d, 0), fill)


def _lru_kernel(p_ref, cw_ref, cb_ref, wa_ref, ba_ref, wx_ref, bx_ref, lam_ref, o_ref, tail_ref, h_ref):
    Lt = p_ref.shape[0]
    W = GROUP_WIDTH

    @pl.when(pl.program_id(1) == 0)
    def _():
        tail_ref[...] = jnp.zeros_like(tail_ref)
        h_ref[...] = jnp.zeros_like(h_ref)

    xb = p_ref[:, :W]
    gate = p_ref[:, W:]
    ext = jnp.concatenate([tail_ref[...], xb], axis=0)
    tail_ref[...] = xb[Lt - SUBLANES:]
    xc = xb * cw_ref[LRU_CONV - 1:LRU_CONV, :] + cb_ref[...]
    for sh in range(1, LRU_CONV):
        xs = pltpu.roll(ext, sh, 0)[SUBLANES:]
        xc = xc + xs * cw_ref[LRU_CONV - 1 - sh:LRU_CONV - sh, :]
    xcb = xc.astype(BF16)
    r = _sigmoid(_dot(xcb, wa_ref[...]) + ba_ref[...])
    ig = _sigmoid(_dot(xcb, wx_ref[...]) + bx_ref[...])
    log_a = -LRU_C * r * _softplus(-lam_ref[...])
    a = jnp.exp(log_a)
    th = jnp.tanh(log_a)
    mult = jnp.sqrt(jnp.maximum(-2.0 * th / (1.0 - th), 1e-12))
    b = mult * (ig * xc)
    d = 1
    while d < Lt:
        b = b + a * _shift_rows(b, d, 0.0)
        a = a * _shift_rows(a, d, 1.0)
        d *= 2
    h = b + a * h_ref[...]
    h_ref[...] = h[Lt - 1:Lt]
    o_ref[...] = (h * _gelu_tanh(gate)).astype(BF16)


def _lru(p, cw, cb, wa, ba, wx, bx, lam):
    B, S, _ = p.shape
    Lt = LT_LRU
    W = GROUP_WIDTH
    return pl.pallas_call(
        _lru_kernel,
        out_shape=jax.ShapeDtypeStruct((B, S, W), BF16),
        grid=(B, S // Lt),
        in_specs=[
            pl.BlockSpec((None, Lt, 2 * W), lambda b, t: (b, t, 0)),
            _const_spec((LRU_CONV, W)),
            _const_spec((1, W)),
            _const_spec((W, W)),
            _const_spec((1, W)),
            _const_spec((W, W)),
            _const_spec((1, W)),
            _const_spec((1, W)),
        ],
        out_specs=pl.BlockSpec((None, Lt, W), lambda b, t: (b, t, 0)),
        scratch_shapes=[pltpu.VMEM((SUBLANES, W), F32), pltpu.VMEM((1, W), F32)],
        compiler_params=_params("parallel", "arbitrary"),
        name="lru",
    )(p, cw, cb, wa, ba, wx, bx, lam)


def _outffn_kernel(x_ref, oa_ref, ob_ref, oc_ref, od_ref, wo_ref, g1_ref, g2_ref, g3_ref,
                   w1_ref, w2_ref, out_ref):
    mixin = jnp.concatenate([oa_ref[...], ob_ref[...], oc_ref[...], od_ref[...]], axis=-1)
    x1 = x_ref[...] + _rms(_dot(mixin, wo_ref[...]), g1_ref[...])
    h = _rms(x1, g2_ref[...]).astype(BF16)
    f = jnp.zeros_like(x1)
    for c in range(D_FF // FF_CHUNK):
        a = jnp.maximum(_dot(h, w1_ref[:, c * FF_CHUNK:(c + 1) * FF_CHUNK]), 0.0)
        f = f + _dot((a * a).astype(BF16), w2_ref[c * FF_CHUNK:(c + 1) * FF_CHUNK, :])
    out_ref[...] = x1 + _rms(f, g3_ref[...])


def _outffn(x, oa, ob, oc, od, wo, g1, g2, g3, w1, w2):
    B, S, D = x.shape
    tm = TM_FFN
    W = GROUP_WIDTH
    row = lambda n: pl.BlockSpec((None, tm, n), lambda b, t: (b, t, 0))
    once = lambda shape: pl.BlockSpec(shape, lambda b, t: (0, 0), pipeline_mode=pl.Buffered(1))
    return pl.pallas_call(
        _outffn_kernel,
        out_shape=jax.ShapeDtypeStruct((B, S, D), F32),
        grid=(B, S // tm),
        in_specs=[row(D), row(W), row(W), pl.BlockSpec((tm, W), lambda b, t: (t, b)), row(W),
                  once((D, D)), _const_spec((1, D)), _const_spec((1, D)), _const_spec((1, D)),
                  once((D, D_FF)), once((D_FF, D))],
        out_specs=row(D),
        compiler_params=_params("parallel", "parallel"),
        name="outffn",
    )(x, oa, ob, oc, od, wo, g1, g2, g3, w1, w2)


def _rope_tables(S):
    inv = ROPE_THETA ** (-jnp.arange(0, DIFF_DH, 2, dtype=F32) / DIFF_DH)
    ang = jnp.arange(S, dtype=F32)[:, None] * inv[None, :]
    cos = jnp.concatenate([jnp.cos(ang), jnp.cos(ang)], axis=-1)
    sin = jnp.concatenate([-jnp.sin(ang), jnp.sin(ang)], axis=-1)
    reps = GROUP_WIDTH // DIFF_DH
    return jnp.tile(cos, (1, reps)), jnp.tile(sin, (1, reps))


def _block_diag(blocks):
    n, a, b = blocks.shape
    eye = jnp.eye(n, dtype=blocks.dtype)
    return (eye[:, None, :, None] * blocks[:, :, None, :]).reshape(n * a, n * b)


def _s5_perm():
    n = SUBLANES * TS_S5
    r = np.arange(n)
    src = (r % SUBLANES) * TS_S5 + r // SUBLANES
    p = np.zeros((n, n), np.float32)
    p[r, src] = 1.0
    return p


def kernel(x, ln_mix_pre, ln_mix_post, ln_ffn_pre, ln_ffn_post, w_in, w_out, gla_w_gate, gla_b_gate, gla_norm, diff_lq1, diff_lk1, diff_lq2, diff_lk2, diff_norm, s5_log_step, s5_a_re, s5_a_im, s5_b_re, s5_b_im, s5_c_re, s5_c_im, s5_d, s5_w_glu, s5_b_glu, lru_conv_w, lru_conv_b, lru_w_a, lru_b_a, lru_w_x, lru_b_x, lru_lambda, ffn_w1, ffn_w2):
    B, S, D = x.shape
    L = w_in.shape[0]
    W = GROUP_WIDTH
    HK = GLA_HEADS * GLA_DK

    cos_n, sin_n = _rope_tables(S)
    q_scale = DIFF_DH ** -0.5 * math.log2(math.e)
    cos_t = (cos_n * q_scale).T
    sin_t = (sin_n * q_scale).T
    tril = jnp.asarray(np.kron(np.eye(LT_GLA // GLA_CHUNK), np.tril(np.ones((GLA_CHUNK, GLA_CHUNK)))), BF16)
    gmean = jnp.asarray(np.kron(np.eye(GLA_HEADS), np.full((GLA_DV, GLA_DV), 1.0 / GLA_DV)), BF16)
    perm = _s5_perm()
    perm_b, permt_b = jnp.asarray(perm, BF16), jnp.asarray(perm.T, BF16)

    ls_e = jnp.repeat(s5_log_step, S5_STATE, axis=1)[:, None, :]
    lr_e = s5_a_re.reshape(L, 1, S5_LANES)
    li_e = s5_a_im.reshape(L, 1, S5_LANES)
    br_e = s5_b_re.transpose(0, 3, 1, 2).reshape(L, S5_CH, S5_LANES)
    bi_e = s5_b_im.transpose(0, 3, 1, 2).reshape(L, S5_CH, S5_LANES)
    bbr, bbi, ab = _s5_prep(ls_e, lr_e, li_e, br_e, bi_e)
    grp = jnp.arange(S5_LANES) // S5_STATE
    gsel = (jnp.arange(S5_GROUPS)[:, None] == grp[None, :]).astype(F32)

    for l in range(L):
        lam_init = 0.8 - 0.6 * math.exp(-0.3 * l)
        wi = w_in[l]
        offs = np.cumsum([0, HK, HK, 2 * HK, GLA_RANK, 2 * HK, W, W, W, W, W, W])
        seg = [wi[:, offs[i]:offs[i + 1]] for i in range(11)]
        g_q, g_k, g_v, g_lr, g_og, d_q, d_k, d_v, s_u, r_x, r_g = seg
        glr_pad = jnp.pad(g_lr, ((0, 0), (0, LANES - GLA_RANK)))
        wn = jnp.concatenate([g_q, g_k, g_v, g_og, glr_pad, d_k, s_u, r_x, r_g], axis=1).astype(BF16)
        wt = jnp.concatenate([d_q, d_v], axis=1).T.astype(BF16)

        gla_p, kd, qt, vt, su, rxg = _inproj(x, ln_mix_pre[l][None], wn, wt, cos_n, sin_n, cos_t, sin_t)

        wg = jnp.pad(gla_w_gate[l], ((0, LANES - GLA_RANK), (0, 0))).astype(BF16)
        o_a = _gla(gla_p, wg, gla_b_gate[l][None], tril, gmean,
                   jnp.tile(gla_norm[l], GLA_HEADS)[None])

        lq = jnp.stack([diff_lq1[l], diff_lq2[l]])
        lk = jnp.stack([diff_lk1[l], diff_lk2[l]])
        gain_t = jnp.broadcast_to(jnp.tile(diff_norm[l], DIFF_HEADS)[:, None], (W, T_ATT))
        o_b = _diff(qt, kd, vt, lq, lk, gain_t, lam_init)

        bb = jnp.concatenate(
            [(gsel[:, None, :] * b[None, :, :]).reshape(W, S5_LANES) for b in (bbr[l], bbi[l])],
            axis=1).astype(BF16)
        cm = jnp.concatenate([_block_diag(s5_c_re[l]).T, -_block_diag(s5_c_im[l]).T], axis=0).astype(BF16)
        o_c = _s5(su, perm_b, permt_b, bb, cm, ab[l], s5_d[l][None],
                  s5_w_glu[l].astype(BF16), s5_b_glu[l][None])

        o_d = _lru(rxg, lru_conv_w[l], lru_conv_b[l][None],
                   _block_diag(lru_w_a[l]).astype(BF16), lru_b_a[l][None],
                   _block_diag(lru_w_x[l]).astype(BF16), lru_b_x[l][None],
                   lru_lambda[l][None])

        x = _outffn(x, o_a, o_b, o_c, o_d, w_out[l].astype(BF16),
                    ln_mix_post[l][None], ln_ffn_pre[l][None], ln_ffn_post[l][None],
                    ffn_w1[l].astype(BF16), ffn_w2[l].astype(BF16))
    return x
```

```python
import functools
import math

import numpy as np
import jax
import jax.numpy as jnp
from jax import lax
from jax.experimental import pallas as pl
from jax.experimental.pallas import tpu as pltpu

F32 = jnp.float32
BF16 = jnp.bfloat16

D_MODEL = 1024
GROUP_WIDTH = 256
EPS = 1e-6
GLA_HEADS, GLA_DK, GLA_DV, GLA_RANK, GLA_CHUNK = 4, 32, 64, 16, 64
GLA_GATE_NORM = 16.0
DIFF_HEADS, DIFF_DH, DIFF_DV = 4, 32, 64
ROPE_THETA = 10000.0
S5_GROUPS, S5_STATE, S5_CH = 16, 64, 16
S5_LANES = S5_GROUPS * S5_STATE
LRU_BLOCKS, LRU_BLOCK, LRU_CONV, LRU_C = 4, 64, 4, 8.0
D_FF = 4 * D_MODEL

SUBLANES = 8
LANES = 128
VMEM_LIMIT = 56 * 1024 * 1024

TM_INPROJ = 512
T_ATT = 256
LT_GLA = 256
NB_GLA = 4
TS_S5 = 32
NSUB_S5 = 4
TM_FFN = 512
FF_CHUNK = 1024


def _dot(a, b):
    return jnp.dot(a, b, preferred_element_type=F32)


def _dot_nt(a, b):
    return lax.dot_general(a, b, (((1,), (1,)), ((), ())), preferred_element_type=F32)


def _dot_tn(a, b):
    return lax.dot_general(a, b, (((0,), (0,)), ((), ())), preferred_element_type=F32)


def _split2(x):
    hi = x.astype(BF16)
    lo = (x - hi.astype(F32)).astype(BF16)
    return hi, lo


def _split3(x):
    hi = x.astype(BF16)
    r = x - hi.astype(F32)
    mid = r.astype(BF16)
    lo = (r - mid.astype(F32)).astype(BF16)
    return hi, mid, lo


def _sigmoid(x):
    return 1.0 / (1.0 + jnp.exp(-x))


def _softplus(x):
    return jnp.maximum(x, 0.0) + jnp.log1p(jnp.exp(-jnp.abs(x)))


def _gelu_tanh(x):
    c = math.sqrt(2.0 / math.pi)
    return 0.5 * x * (1.0 + jnp.tanh(c * (x + 0.044715 * (x * x * x))))


def _rms(x, g):
    ms = jnp.mean(x * x, axis=-1, keepdims=True)
    return x * lax.rsqrt(ms + EPS) * g


def _params(*sem):
    return pltpu.CompilerParams(dimension_semantics=sem, vmem_limit_bytes=VMEM_LIMIT)


def _const_spec(shape):
    nd = len(shape)
    return pl.BlockSpec(shape, lambda *_: (0,) * nd)


N_GLA = 896
N_NAT = N_GLA + 4 * GROUP_WIDTH
N_T = 2 * GROUP_WIDTH


def _shift_rows(x, d, fill):
    n = x.shape[0]
    if d % SUBLANES == 0:
        return jnp.concatenate([jnp.full((d, x.shape[1]), fill, x.dtype), x[:n - d]], axis=0)
    r = lax.broadcasted_iota(jnp.int32, x.shape, 0)
    return jnp.where(r >= d, pltpu.roll(x, d, 0), fill)


def _inproj_kernel(x_ref, g_ref, wn_ref, wt_ref, cn_ref, sn_ref, ct_ref, st_ref,
                   cw_ref, cb_ref, wa_ref, ba_ref, wx_ref, bx_ref, lam_ref,
                   gla_ref, kd_ref, qt_ref, vt_ref, su_ref, od_ref, tail_ref, hst_ref):
    tm = x_ref.shape[0]
    W = GROUP_WIDTH
    HALF = DIFF_DH // 2

    @pl.when(pl.program_id(1) == 0)
    def _():
        tail_ref[...] = jnp.zeros_like(tail_ref)
        hst_ref[...] = jnp.zeros_like(hst_ref)

    h = _rms(x_ref[...], g_ref[...]).astype(BF16)
    o = N_GLA
    rxg = _dot(h, wn_ref[:, o + 2 * W:o + 4 * W])
    xb, gate = rxg[:, :W], rxg[:, W:]
    ext = jnp.concatenate([tail_ref[...], xb], axis=0)
    tail_ref[...] = xb[tm - SUBLANES:]
    xc = xb * cw_ref[LRU_CONV - 1:LRU_CONV, :] + cb_ref[...]
    for sh in range(1, LRU_CONV):
        xs = pltpu.roll(ext, sh, 0)[SUBLANES:]
        xc = xc + xs * cw_ref[LRU_CONV - 1 - sh:LRU_CONV - sh, :]
    xcb = xc.astype(BF16)

    k = _dot(h, wn_ref[:, o:o + W])
    first_half = lax.broadcasted_iota(jnp.int32, k.shape, 1) % DIFF_DH < HALF
    k_sw = jnp.where(first_half, pltpu.roll(k, W - HALF, 1), pltpu.roll(k, HALF, 1))
    kd_ref[...] = (k * cn_ref[...] + k_sw * sn_ref[...]).astype(BF16)

    r = _sigmoid(_dot(xcb, wa_ref[...]) + ba_ref[...])
    ig = _sigmoid(_dot(xcb, wx_ref[...]) + bx_ref[...])

    log_a = -LRU_C * r * _softplus(-lam_ref[...])
    a = jnp.exp(log_a)
    th = jnp.tanh(log_a)
    mult = jnp.sqrt(jnp.maximum(-2.0 * th / (1.0 - th), 1e-12))
    b = mult * (ig * xc)

    def scan_steps(a, b, d, n):
        for _ in range(n):
            if d < tm:
                b = b + a * _shift_rows(b, d, 0.0)
                a = a * _shift_rows(a, d, 1.0)
                d *= 2
        return a, b, d

    a, b, d = scan_steps(a, b, 1, 2)
    for c in range(0, N_GLA, 2 * W):
        gla_ref[:, c:min(c + 2 * W, N_GLA)] = _dot(h, wn_ref[:, c:min(c + 2 * W, N_GLA)])
        a, b, d = scan_steps(a, b, d, 1)
    su_ref[...] = _dot(h, wn_ref[:, o + W:o + 2 * W])
    a, b, d = scan_steps(a, b, d, 1)
    q = _dot_nt(wt_ref[:W, :], h)
    q_sw = jnp.concatenate(
        [q[g * DIFF_DH + off:g * DIFF_DH + off + HALF] for g in range(W // DIFF_DH) for off in (HALF, 0)],
        axis=0)
    qt_ref[...] = (q * ct_ref[...] + q_sw * st_ref[...]).astype(BF16)
    a, b, d = scan_steps(a, b, d, 16)
    hs = b + a * hst_ref[...]
    hst_ref[...] = hs[tm - 1:tm]
    gated = hs * _gelu_tanh(gate)
    vt = _dot_nt(wt_ref[W:, :], h).astype(BF16)
    for j in range(vt_ref.shape[0]):
        vt_ref[j] = vt[:, j * T_ATT:(j + 1) * T_ATT]
    od_ref[...] = gated.astype(BF16)


def _inproj(x, g, wn, wt, cn, sn, ct, st, cw, cb, wa, ba, wx, bx, lam):
    B, S, D = x.shape
    tm = TM_INPROJ
    W = GROUP_WIDTH
    nsub = tm // T_ATT
    out_shape = (
        jax.ShapeDtypeStruct((B, S, N_GLA), F32),
        jax.ShapeDtypeStruct((B, S, W), BF16),
        jax.ShapeDtypeStruct((B, W, S), BF16),
        jax.ShapeDtypeStruct((B, S // T_ATT, W, T_ATT), BF16),
        jax.ShapeDtypeStruct((S, B * W), F32),
        jax.ShapeDtypeStruct((B, S, W), BF16),
    )
    row = lambda n: pl.BlockSpec((None, tm, n), lambda b, t: (b, t, 0))
    return pl.pallas_call(
        _inproj_kernel,
        out_shape=out_shape,
        grid=(B, S // tm),
        in_specs=[
            row(D),
            _const_spec((1, D)),
            _const_spec((D, N_NAT)),
            _const_spec((N_T, D)),
            pl.BlockSpec((tm, W), lambda b, t: (t, 0)),
            pl.BlockSpec((tm, W), lambda b, t: (t, 0)),
            pl.BlockSpec((W, tm), lambda b, t: (0, t)),
            pl.BlockSpec((W, tm), lambda b, t: (0, t)),
            _const_spec((LRU_CONV, W)),
            _const_spec((1, W)),
            _const_spec((W, W)),
            _const_spec((1, W)),
            _const_spec((W, W)),
            _const_spec((1, W)),
            _const_spec((1, W)),
        ],
        out_specs=(
            row(N_GLA),
            row(W),
            pl.BlockSpec((None, W, tm), lambda b, t: (b, 0, t)),
            pl.BlockSpec((None, nsub, W, T_ATT), lambda b, t: (b, t, 0, 0)),
            pl.BlockSpec((tm, W), lambda b, t: (t, b)),
            row(W),
        ),
        scratch_shapes=[pltpu.VMEM((SUBLANES, W), F32), pltpu.VMEM((1, W), F32)],
        compiler_params=_params("parallel", "arbitrary"),
        name="inproj",
    )(x, g, wn, wt, cn, sn, ct, st, cw, cb, wa, ba, wx, bx, lam)


def _gla_kernel(p_ref, wg_ref, bg_ref, tril_ref, gm_ref, gain_ref, o_ref, st_ref):
    NB, Lt, _ = p_ref.shape
    C = GLA_CHUNK
    HK = GLA_HEADS * GLA_DK
    HV = GLA_HEADS * GLA_DV

    @pl.when(pl.program_id(1) == 0)
    def _():
        st_ref[...] = jnp.zeros_like(st_ref)

    lane_k = lax.broadcasted_iota(jnp.int32, (C, HK), 1) // GLA_DK
    r_a = lax.broadcasted_iota(jnp.int32, (GLA_HEADS * C, C), 0) % C
    c_a = lax.broadcasted_iota(jnp.int32, (GLA_HEADS * C, C), 1)
    causal = r_a >= c_a
    lane_v = lax.broadcasted_iota(jnp.int32, (C, HV), 1) // GLA_DV
    st_row = lax.broadcasted_iota(jnp.int32, (HV, HK), 0) // GLA_DV
    st_col = lax.broadcasted_iota(jnp.int32, (HV, HK), 1) // GLA_DK
    same_head = st_row == st_col
    tril = tril_ref[...]

    q_dec, k_dec, k_raw, v, bcum = [], [], [], [], []
    for b in range(NB):
        k = p_ref[b, :, HK:2 * HK]
        glr = p_ref[b, :, 2 * HK + 2 * HV:].astype(BF16)
        g = -_softplus(-(_dot(glr, wg_ref[...]) + bg_ref[...])) * (1.0 / GLA_GATE_NORM)
        g_hi, g_lo = _split2(g)
        bc = _dot(tril, g_hi) + _dot(tril, g_lo)
        q_dec.append((p_ref[b, :, 0:HK] * (GLA_DK ** -0.5) * jnp.exp(bc)).astype(BF16))
        k_dec.append((k * jnp.exp(-bc)).astype(BF16))
        k_raw.append(k)
        v.append(p_ref[b, :, 2 * HK:2 * HK + HV].astype(BF16))
        bcum.append(bc)

    outs = [[] for _ in range(NB)]
    for c in range(Lt // C):
        sl = slice(c * C, (c + 1) * C)
        for b in range(NB):
            qd, kd, vc = q_dec[b][sl], k_dec[b][sl], v[b][sl]
            bc = bcum[b][sl]
            blast = bc[C - 1:C, :]
            k_st = (k_raw[b][sl] * jnp.exp(blast - bc)).astype(BF16)
            q_big = jnp.concatenate(
                [jnp.where(lane_k == h, qd, jnp.zeros_like(qd)) for h in range(GLA_HEADS)], axis=0)
            attn = jnp.where(causal, _dot_nt(q_big, kd), 0.0).astype(BF16)
            o_all = _dot(attn, vc)
            o_c = jnp.zeros((C, HV), F32)
            for h in range(GLA_HEADS):
                o_c = o_c + jnp.where(lane_v == h, o_all[h * C:(h + 1) * C], 0.0)
            state = st_ref[b]
            o_c = o_c + _dot_nt(qd, state.astype(BF16))
            kv_t = jnp.where(same_head, _dot_tn(vc, k_st), 0.0)
            st_ref[b] = state * jnp.exp(blast) + kv_t
            outs[b].append(o_c)
    for b in range(NB):
        o = jnp.concatenate(outs[b], axis=0)
        sq_hi, sq_lo = _split2(o * o)
        ms = _dot(sq_hi, gm_ref[...]) + _dot(sq_lo, gm_ref[...])
        o = o * lax.rsqrt(ms + EPS) * gain_ref[...]
        og = p_ref[b, :, 2 * HK + HV:2 * HK + 2 * HV]
        o_ref[b] = (o * (og * _sigmoid(og))).astype(BF16)


def _gla(p, wg, bg, tril, gm, gain):
    B, S, _ = p.shape
    Lt = LT_GLA
    NB = NB_GLA
    HV = GLA_HEADS * GLA_DV
    HK = GLA_HEADS * GLA_DK
    return pl.pallas_call(
        _gla_kernel,
        out_shape=jax.ShapeDtypeStruct((B, S, HV), BF16),
        grid=(B // NB, S // Lt),
        in_specs=[
            pl.BlockSpec((NB, Lt, N_GLA), lambda b, t: (b, t, 0)),
            _const_spec((HK, HK)),
            _const_spec((1, HK)),
            _const_spec((Lt, Lt)),
            _const_spec((HV, HV)),
            _const_spec((1, HV)),
        ],
        out_specs=pl.BlockSpec((NB, Lt, HV), lambda b, t: (b, t, 0)),
        scratch_shapes=[pltpu.VMEM((NB, HV, HK), F32)],
        compiler_params=_params("parallel", "arbitrary"),
        name="gla",
    )(p, wg, bg, tril, gm, gain)


V_AUG = DIFF_DV + 16

def _diff_kernel(qt_ref, k_ref, vt_ref, lq_ref, lk_ref, gain_ref, o_ref, wq_ref, acc_ref, m_ref, s_ref,
                 mx_ref, *, lam_init):
    T = T_ATT
    NB = 2 * DIFF_HEADS
    i = pl.program_id(1)
    qt = qt_ref[...]
    row = lax.broadcasted_iota(jnp.int32, qt.shape, 0) // DIFF_DH
    for n in range(NB):
        wq_ref[:, n * T:(n + 1) * T] = jnp.where(row == n, qt, jnp.zeros_like(qt))
    acc_ref[...] = jnp.zeros_like(acc_ref)
    m_ref[...] = jnp.full(m_ref.shape, -1e30, F32)
    ones = jnp.ones((V_AUG - DIFF_DV, T), BF16)

    def produce(j, h, buf, diagonal):
        kb = k_ref[pl.ds(pl.multiple_of(j * T, T), T), :]
        s = _dot(kb, wq_ref[:, 2 * h * T:(2 * h + 2) * T])
        if diagonal:
            kk = lax.broadcasted_iota(jnp.int32, s.shape, 0)
            qq = lax.broadcasted_iota(jnp.int32, s.shape, 1)
            s = jnp.where(kk <= jnp.where(qq >= T, qq - T, qq), s, -1e30)
        s_ref[buf, h] = s
        mx_ref[buf, h] = jnp.max(s, axis=0, keepdims=True)

    def block(j, j_next, cons, prod):
        for h in range(DIFF_HEADS):
            produce(j_next, h, prod, False)
            va = jnp.concatenate([vt_ref[j, h * DIFF_DV:(h + 1) * DIFF_DV, :], ones], axis=0)
            for c in range(2):
                n = 2 * h + c
                m = m_ref[n]
                m_new = jnp.maximum(m, mx_ref[cons, h, :, c * T:(c + 1) * T])
                p = jnp.exp2(s_ref[cons, h, :, c * T:(c + 1) * T] - m_new).astype(BF16)
                acc_ref[n] = jnp.exp2(m - m_new) * acc_ref[n] + _dot(va, p)
                m_ref[n] = m_new

    for h in range(DIFF_HEADS):
        produce(i, h, 0, True)
    block(i, 0, 0, 1)

    def body(t, carry):
        block(2 * t, 2 * t + 1, 1, 0)
        block(2 * t + 1, jnp.minimum(2 * t + 2, i - 1), 0, 1)
        return carry

    lax.fori_loop(0, i >> 1, body, 0)

    @pl.when((i & 1) == 1)
    def _():
        block(i - 1, i - 1, 1, 0)

    e = jnp.exp(jnp.sum(lq_ref[...] * lk_ref[...], axis=-1, keepdims=True))
    lam = e[0:1] - e[1:2] + lam_init
    heads = []
    for h in range(DIFF_HEADS):
        a1, a2 = acc_ref[2 * h], acc_ref[2 * h + 1]
        o1 = a1[:DIFF_DV] / a1[DIFF_DV:DIFF_DV + 1]
        o2 = a2[:DIFF_DV] / a2[DIFF_DV:DIFF_DV + 1]
        oh = o1 - lam * o2
        ms = jnp.mean(oh * oh, axis=0, keepdims=True)
        heads.append(oh * lax.rsqrt(ms + EPS))
    ot = jnp.concatenate(heads, axis=0) * gain_ref[...] * (1.0 - lam_init)
    o_ref[...] = ot.T.astype(BF16)


def _diff(qt, k, vt, lq, lk, gain_t, lam_init):
    B, W, S = qt.shape
    T = T_ATT
    return pl.pallas_call(
        functools.partial(_diff_kernel, lam_init=lam_init),
        out_shape=jax.ShapeDtypeStruct((B, S, W), BF16),
        grid=(B, S // T),
        in_specs=[
            pl.BlockSpec((None, W, T), lambda b, i: (b, 0, i)),
            pl.BlockSpec((None, S, W), lambda b, i: (b, 0, 0)),
            pl.BlockSpec((None, S // T, W, T), lambda b, i: (b, 0, 0, 0)),
            _const_spec((2, DIFF_DH)),
            _const_spec((2, DIFF_DH)),
            _const_spec((W, T)),
        ],
        out_specs=pl.BlockSpec((None, T, W), lambda b, i: (b, i, 0)),
        scratch_shapes=[pltpu.VMEM((W, 2 * DIFF_HEADS * T), BF16),
                        pltpu.VMEM((2 * DIFF_HEADS, V_AUG, T), F32),
                        pltpu.VMEM((2 * DIFF_HEADS, 1, T), F32),
                        pltpu.VMEM((2, DIFF_HEADS, T, 2 * T), F32),
                        pltpu.VMEM((2, DIFF_HEADS, 1, 2 * T), F32)],
        compiler_params=_params("parallel", "arbitrary"),
        name="diff",
    )(qt, k, vt, lq, lk, gain_t)


def _s5_prep_kernel(ls_ref, lr_ref, li_ref, br_ref, bi_ref,
                    bbr_ref, bbi_ref, ab_ref):
    step = jnp.exp(ls_ref[...])
    lr, li = lr_ref[...], li_ref[...]
    mag = jnp.exp(lr * step)
    ab_re = mag * jnp.cos(li * step)
    ab_im = mag * jnp.sin(li * step)
    den = lr * lr + li * li
    nr, ni = ab_re - 1.0, ab_im
    cr = (nr * lr + ni * li) / den
    ci = (ni * lr - nr * li) / den
    bbr_ref[...] = cr * br_ref[...] - ci * bi_ref[...]
    bbi_ref[...] = cr * bi_ref[...] + ci * br_ref[...]

    ab_ref[0] = jnp.broadcast_to(ab_re, (SUBLANES, S5_LANES))
    ab_ref[1] = jnp.broadcast_to(ab_im, (SUBLANES, S5_LANES))


def _s5_prep(ls_e, lr, li, br_e, bi_e):
    L = ls_e.shape[0]
    N = S5_LANES
    vec = pl.BlockSpec((None, 1, N), lambda l: (l, 0, 0))
    mat = pl.BlockSpec((None, S5_CH, N), lambda l: (l, 0, 0))
    return pl.pallas_call(
        _s5_prep_kernel,
        out_shape=(
            jax.ShapeDtypeStruct((L, S5_CH, N), F32),
            jax.ShapeDtypeStruct((L, S5_CH, N), F32),
            jax.ShapeDtypeStruct((L, 2, SUBLANES, N), F32),
        ),
        grid=(L,),
        in_specs=[vec, vec, vec, mat, mat],
        out_specs=(mat, mat, pl.BlockSpec((None, 2, SUBLANES, N), lambda l: (l, 0, 0, 0))),
        compiler_params=_params("arbitrary"),
        name="s5_prep",
    )(ls_e, lr, li, br_e, bi_e)


def _s5_kernel(u_ref, perm_ref, permt_ref, bb_ref, cm_ref, ab_ref, d_ref, wglu_ref, bglu_ref,
               o_ref, z_ref, st_ref):
    N = S5_LANES
    W = GROUP_WIDTH
    TS = TS_S5
    nsub = u_ref.shape[0] // TS

    @pl.when(pl.program_id(0) == 0)
    def _():
        st_ref[...] = jnp.zeros_like(st_ref)

    def slab(k):
        return jnp.concatenate(
            [u_ref[k * TS:(k + 1) * TS, b * W:(b + 1) * W] for b in range(SUBLANES)], axis=0)

    def project(k):
        u_tb = _dot(perm_ref[...], slab(k).astype(BF16)).astype(BF16)
        z_ref[k] = _dot(u_tb, bb_ref[...])

    ar, ai = ab_ref[0], ab_ref[1]
    project(0)
    if nsub > 1:
        project(1)
    for k in range(nsub):
        zk = z_ref.at[k]
        zr, zi = st_ref[0], st_ref[1]
        for t in range(TS):
            rs = slice(t * SUBLANES, (t + 1) * SUBLANES)
            zr, zi = (ar * zr - ai * zi + zk[rs, :N], ar * zi + ai * zr + zk[rs, N:])
            zk[rs, :N] = zr
            zk[rs, N:] = zi
        st_ref[0] = zr
        st_ref[1] = zi

        half = SUBLANES * TS // 2
        y_tb = jnp.concatenate([_dot(zk[:half, :].astype(BF16), cm_ref[...]),
                                _dot(zk[half:, :].astype(BF16), cm_ref[...])], axis=0)
        if k + 2 < nsub:
            project(k + 2)
        y_hi, y_lo = _split2(y_tb)
        y = _dot(permt_ref[...], y_hi) + _dot(permt_ref[...], y_lo)
        y = _gelu_tanh(y + d_ref[...] * slab(k))
        gate = _sigmoid(_dot(y.astype(BF16), wglu_ref[...]) + bglu_ref[...])
        out = (y * gate).astype(BF16)
        for b in range(SUBLANES):
            o_ref[k * TS:(k + 1) * TS, b * W:(b + 1) * W] = out[b * TS:(b + 1) * TS]


def _s5(u, perm, permt, bb, cm, ab, d, wglu, bglu):
    S, BW = u.shape
    W = GROUP_WIDTH
    assert BW == SUBLANES * W, "the S5 kernel maps the batch onto the 8 sublanes"
    rows = SUBLANES * TS_S5
    Lt = TS_S5 * NSUB_S5
    N = S5_LANES
    return pl.pallas_call(
        _s5_kernel,
        out_shape=jax.ShapeDtypeStruct((S, BW), BF16),
        grid=(S // Lt,),
        in_specs=[
            pl.BlockSpec((Lt, BW), lambda t: (t, 0)),
            _const_spec((rows, rows)),
            _const_spec((rows, rows)),
            _const_spec((W, 2 * N)),
            _const_spec((2 * N, W)),
            _const_spec((2, SUBLANES, N)),
            _const_spec((1, W)),
            _const_spec((W, W)),
            _const_spec((1, W)),
        ],
        out_specs=pl.BlockSpec((Lt, BW), lambda t: (t, 0)),
        scratch_shapes=[pltpu.VMEM((NSUB_S5, rows, 2 * N), F32), pltpu.VMEM((2, SUBLANES, N), F32)],
        compiler_params=_params("arbitrary"),
        name="s5",
    )(u, perm, permt, bb, cm, ab, d, wglu, bglu)


def _outffn_kernel(x_ref, oa_ref, ob_ref, oc_ref, od_ref, wo_ref, g1_ref, g2_ref, g3_ref,
                   w1_ref, w2_ref, out_ref):
    mixin = jnp.concatenate([oa_ref[...], ob_ref[...], oc_ref[...], od_ref[...]], axis=-1)
    x1 = x_ref[...] + _rms(_dot(mixin, wo_ref[...]), g1_ref[...])
    h = _rms(x1, g2_ref[...]).astype(BF16)
    f = jnp.zeros_like(x1)
    for c in range(D_FF // FF_CHUNK):
        a = jnp.maximum(_dot(h, w1_ref[:, c * FF_CHUNK:(c + 1) * FF_CHUNK]), 0.0)
        f = f + _dot((a * a).astype(BF16), w2_ref[c * FF_CHUNK:(c + 1) * FF_CHUNK, :])
    out_ref[...] = x1 + _rms(f, g3_ref[...])


def _outffn(x, oa, ob, oc, od, wo, g1, g2, g3, w1, w2):
    B, S, D = x.shape
    tm = TM_FFN
    W = GROUP_WIDTH
    row = lambda n: pl.BlockSpec((None, tm, n), lambda b, t: (b, t, 0))
    once = lambda shape: pl.BlockSpec(shape, lambda b, t: (0, 0), pipeline_mode=pl.Buffered(1))
    return pl.pallas_call(
        _outffn_kernel,
        out_shape=jax.ShapeDtypeStruct((B, S, D), F32),
        grid=(B, S // tm),
        in_specs=[row(D), row(W), row(W), pl.BlockSpec((tm, W), lambda b, t: (t, b)), row(W),
                  once((D, D)), _const_spec((1, D)), _const_spec((1, D)), _const_spec((1, D)),
                  once((D, D_FF)), once((D_FF, D))],
        out_specs=row(D),
        compiler_params=_params("parallel", "parallel"),
        name="outffn",
    )(x, oa, ob, oc, od, wo, g1, g2, g3, w1, w2)


def _rope_tables(S):
    inv = ROPE_THETA ** (-jnp.arange(0, DIFF_DH, 2, dtype=F32) / DIFF_DH)
    ang = jnp.arange(S, dtype=F32)[:, None] * inv[None, :]
    cos = jnp.concatenate([jnp.cos(ang), jnp.cos(ang)], axis=-1)
    sin = jnp.concatenate([-jnp.sin(ang), jnp.sin(ang)], axis=-1)
    reps = GROUP_WIDTH // DIFF_DH
    return jnp.tile(cos, (1, reps)), jnp.tile(sin, (1, reps))


def _block_diag(blocks):
    n, a, b = blocks.shape
    eye = jnp.eye(n, dtype=blocks.dtype)
    return (eye[:, None, :, None] * blocks[:, :, None, :]).reshape(n * a, n * b)


def _s5_perm():
    n = SUBLANES * TS_S5
    r = np.arange(n)
    src = (r % SUBLANES) * TS_S5 + r // SUBLANES
    p = np.zeros((n, n), np.float32)
    p[r, src] = 1.0
    return p


def kernel(x, ln_mix_pre, ln_mix_post, ln_ffn_pre, ln_ffn_post, w_in, w_out, gla_w_gate, gla_b_gate, gla_norm, diff_lq1, diff_lk1, diff_lq2, diff_lk2, diff_norm, s5_log_step, s5_a_re, s5_a_im, s5_b_re, s5_b_im, s5_c_re, s5_c_im, s5_d, s5_w_glu, s5_b_glu, lru_conv_w, lru_conv_b, lru_w_a, lru_b_a, lru_w_x, lru_b_x, lru_lambda, ffn_w1, ffn_w2):
    B, S, D = x.shape
    L = w_in.shape[0]
    W = GROUP_WIDTH
    HK = GLA_HEADS * GLA_DK

    cos_n, sin_n = _rope_tables(S)
    q_scale = DIFF_DH ** -0.5 * math.log2(math.e)
    cos_t = (cos_n * q_scale).T
    sin_t = (sin_n * q_scale).T
    tril = jnp.asarray(np.kron(np.eye(LT_GLA // GLA_CHUNK), np.tril(np.ones((GLA_CHUNK, GLA_CHUNK)))), BF16)
    gmean = jnp.asarray(np.kron(np.eye(GLA_HEADS), np.full((GLA_DV, GLA_DV), 1.0 / GLA_DV)), BF16)
    perm = _s5_perm()
    perm_b, permt_b = jnp.asarray(perm, BF16), jnp.asarray(perm.T, BF16)

    ls_e = jnp.repeat(s5_log_step, S5_STATE, axis=1)[:, None, :]
    lr_e = s5_a_re.reshape(L, 1, S5_LANES)
    li_e = s5_a_im.reshape(L, 1, S5_LANES)
    br_e = s5_b_re.transpose(0, 3, 1, 2).reshape(L, S5_CH, S5_LANES)
    bi_e = s5_b_im.transpose(0, 3, 1, 2).reshape(L, S5_CH, S5_LANES)
    bbr, bbi, ab = _s5_prep(ls_e, lr_e, li_e, br_e, bi_e)
    grp = jnp.arange(S5_LANES) // S5_STATE
    gsel = (jnp.arange(S5_GROUPS)[:, None] == grp[None, :]).astype(F32)

    for l in range(L):
        lam_init = 0.8 - 0.6 * math.exp(-0.3 * l)
        wi = w_in[l]
        offs = np.cumsum([0, HK, HK, 2 * HK, GLA_RANK, 2 * HK, W, W, W, W, W, W])
        seg = [wi[:, offs[i]:offs[i + 1]] for i in range(11)]
        g_q, g_k, g_v, g_lr, g_og, d_q, d_k, d_v, s_u, r_x, r_g = seg
        glr_pad = jnp.pad(g_lr, ((0, 0), (0, LANES - GLA_RANK)))
        wn = jnp.concatenate([g_q, g_k, g_v, g_og, glr_pad, d_k, s_u, r_x, r_g], axis=1).astype(BF16)
        wt = jnp.concatenate([d_q, d_v], axis=1).T.astype(BF16)

        gla_p, kd, qt, vt, su, o_d = _inproj(
            x, ln_mix_pre[l][None], wn, wt, cos_n, sin_n, cos_t, sin_t,
            lru_conv_w[l], lru_conv_b[l][None],
            _block_diag(lru_w_a[l]).astype(BF16), lru_b_a[l][None],
            _block_diag(lru_w_x[l]).astype(BF16), lru_b_x[l][None], lru_lambda[l][None])

        wg = jnp.pad(gla_w_gate[l], ((0, LANES - GLA_RANK), (0, 0))).astype(BF16)
        o_a = _gla(gla_p, wg, gla_b_gate[l][None], tril, gmean,
                   jnp.tile(gla_norm[l], GLA_HEADS)[None])

        lq = jnp.stack([diff_lq1[l], diff_lq2[l]])
        lk = jnp.stack([diff_lk1[l], diff_lk2[l]])
        gain_t = jnp.broadcast_to(jnp.tile(diff_norm[l], DIFF_HEADS)[:, None], (W, T_ATT))
        o_b = _diff(qt, kd, vt, lq, lk, gain_t, lam_init)

        bb = jnp.concatenate(
            [(gsel[:, None, :] * b[None, :, :]).reshape(W, S5_LANES) for b in (bbr[l], bbi[l])],
            axis=1).astype(BF16)
        cm = jnp.concatenate([_block_diag(s5_c_re[l]).T, -_block_diag(s5_c_im[l]).T], axis=0).astype(BF16)
        o_c = _s5(su, perm_b, permt_b, bb, cm, ab[l], s5_d[l][None],
                  s5_w_glu[l].astype(BF16), s5_b_glu[l][None])

        x = _outffn(x, o_a, o_b, o_c, o_d, w_out[l].astype(BF16),
                    ln_mix_post[l][None], ln_ffn_pre[l][None], ln_ffn_post[l][None],
                    ffn_w1[l].astype(BF16), ffn_w2[l].astype(BF16))
    return x
```

```python
import functools
import math

import numpy as np
import jax
import jax.numpy as jnp
from jax import lax
from jax.experimental import pallas as pl
from jax.experimental.pallas import tpu as pltpu

F32 = jnp.float32
BF16 = jnp.bfloat16

D_MODEL = 1024
GROUP_WIDTH = 256
EPS = 1e-6
GLA_HEADS, GLA_DK, GLA_DV, GLA_RANK, GLA_CHUNK = 4, 32, 64, 16, 64
GLA_GATE_NORM = 16.0
DIFF_HEADS, DIFF_DH, DIFF_DV = 4, 32, 64
ROPE_THETA = 10000.0
S5_GROUPS, S5_STATE, S5_CH = 16, 64, 16
S5_LANES = S5_GROUPS * S5_STATE
LRU_BLOCKS, LRU_BLOCK, LRU_CONV, LRU_C = 4, 64, 4, 8.0
D_FF = 4 * D_MODEL

SUBLANES = 8
LANES = 128
VMEM_LIMIT = 56 * 1024 * 1024

TM_INPROJ = 512
LRU_SCAN_ROWS = 64
T_ATT = 512
LT_GLA = 256
NB_GLA = 4
TS_S5 = 32
NSUB_S5 = 4
TM_FFN = 512
FF_CHUNK = 1024


def _dot(a, b):
    return jnp.dot(a, b, preferred_element_type=F32)


def _dot_nt(a, b):
    return lax.dot_general(a, b, (((1,), (1,)), ((), ())), preferred_element_type=F32)


def _dot_tn(a, b):
    return lax.dot_general(a, b, (((0,), (0,)), ((), ())), preferred_element_type=F32)


def _split2(x):
    hi = x.astype(BF16)
    lo = (x - hi.astype(F32)).astype(BF16)
    return hi, lo


def _split3(x):
    hi = x.astype(BF16)
    r = x - hi.astype(F32)
    mid = r.astype(BF16)
    lo = (r - mid.astype(F32)).astype(BF16)
    return hi, mid, lo


def _sigmoid(x):
    return 1.0 / (1.0 + jnp.exp(-x))


def _softplus(x):
    return jnp.maximum(x, 0.0) + jnp.log1p(jnp.exp(-jnp.abs(x)))


def _gelu_tanh(x):
    c = math.sqrt(2.0 / math.pi)
    return 0.5 * x * (1.0 + jnp.tanh(c * (x + 0.044715 * (x * x * x))))


def _rms(x, g):
    ms = jnp.mean(x * x, axis=-1, keepdims=True)
    return x * lax.rsqrt(ms + EPS) * g


def _params(*sem):
    return pltpu.CompilerParams(dimension_semantics=sem, vmem_limit_bytes=VMEM_LIMIT)


def _const_spec(shape):
    nd = len(shape)
    return pl.BlockSpec(shape, lambda *_: (0,) * nd)


N_GLA = 896
N_NAT = N_GLA + 4 * GROUP_WIDTH
N_T = 2 * GROUP_WIDTH


def _shift_rows(x, d, fill):
    n = x.shape[0]
    if d % SUBLANES == 0:
        return jnp.concatenate([jnp.full((d, x.shape[1]), fill, x.dtype), x[:n - d]], axis=0)
    r = lax.broadcasted_iota(jnp.int32, x.shape, 0)
    return jnp.where(r >= d, pltpu.roll(x, d, 0), fill)


def _inproj_kernel(x_ref, g_ref, wn_ref, wt_ref, cn_ref, sn_ref, ct_ref, st_ref,
                   cw_ref, cb_ref, wa_ref, ba_ref, wx_ref, bx_ref, lam_ref,
                   gla_ref, kd_ref, qt_ref, vt_ref, su_ref, od_ref, tail_ref, hst_ref):
    tm = x_ref.shape[0]
    W = GROUP_WIDTH
    HALF = DIFF_DH // 2

    @pl.when(pl.program_id(1) == 0)
    def _():
        tail_ref[...] = jnp.zeros_like(tail_ref)
        hst_ref[...] = jnp.zeros_like(hst_ref)

    h = _rms(x_ref[...], g_ref[...]).astype(BF16)
    o = N_GLA
    rxg = _dot(h, wn_ref[:, o + 2 * W:o + 4 * W])
    xb, gate = rxg[:, :W], rxg[:, W:]
    ext = jnp.concatenate([tail_ref[...], xb], axis=0)
    tail_ref[...] = xb[tm - SUBLANES:]
    xc = xb * cw_ref[LRU_CONV - 1:LRU_CONV, :] + cb_ref[...]
    for sh in range(1, LRU_CONV):
        xs = pltpu.roll(ext, sh, 0)[SUBLANES:]
        xc = xc + xs * cw_ref[LRU_CONV - 1 - sh:LRU_CONV - sh, :]
    xcb = xc.astype(BF16)

    k = _dot(h, wn_ref[:, o:o + W])
    first_half = lax.broadcasted_iota(jnp.int32, k.shape, 1) % DIFF_DH < HALF
    k_sw = jnp.where(first_half, pltpu.roll(k, W - HALF, 1), pltpu.roll(k, HALF, 1))
    kd_ref[...] = (k * cn_ref[...] + k_sw * sn_ref[...]).astype(BF16)

    r = _sigmoid(_dot(xcb, wa_ref[...]) + ba_ref[...])
    ig = _sigmoid(_dot(xcb, wx_ref[...]) + bx_ref[...])

    gla_ref[...] = _dot(h, wn_ref[:, :N_GLA])
    su_ref[...] = _dot(h, wn_ref[:, o + W:o + 2 * W])
    q = _dot_nt(wt_ref[:W, :], h)
    q_sw = jnp.concatenate(
        [q[g * DIFF_DH + off:g * DIFF_DH + off + HALF] for g in range(W // DIFF_DH) for off in (HALF, 0)],
        axis=0)
    qt_ref[...] = (q * ct_ref[...] + q_sw * st_ref[...]).astype(BF16)
    vt = _dot_nt(wt_ref[W:, :], h).astype(BF16)
    for j in range(vt_ref.shape[0]):
        vt_ref[j] = vt[:, j * T_ATT:(j + 1) * T_ATT]

    sp = _softplus(-lam_ref[...])
    h_prev = hst_ref[...]
    for s0 in range(0, tm, LRU_SCAN_ROWS):
        rows = slice(s0, s0 + LRU_SCAN_ROWS)
        log_a = -LRU_C * r[rows] * sp
        a = jnp.exp(log_a)
        th = jnp.tanh(log_a)
        mult = jnp.sqrt(jnp.maximum(-2.0 * th / (1.0 - th), 1e-12))
        b = mult * (ig[rows] * xc[rows])
        d = 1
        while d < LRU_SCAN_ROWS:
            b = b + a * _shift_rows(b, d, 0.0)
            a = a * _shift_rows(a, d, 1.0)
            d *= 2
        hs = b + a * h_prev
        h_prev = hs[LRU_SCAN_ROWS - 1:LRU_SCAN_ROWS]
        od_ref[rows, :] = (hs * _gelu_tanh(gate[rows])).astype(BF16)
    hst_ref[...] = h_prev


def _inproj(x, g, wn, wt, cn, sn, ct, st, cw, cb, wa, ba, wx, bx, lam):
    B, S, D = x.shape
    tm = TM_INPROJ
    W = GROUP_WIDTH
    nsub = tm // T_ATT
    out_shape = (
        jax.ShapeDtypeStruct((B, S, N_GLA), F32),
        jax.ShapeDtypeStruct((B, S, W), BF16),
        jax.ShapeDtypeStruct((B, W, S), BF16),
        jax.ShapeDtypeStruct((B, S // T_ATT, W, T_ATT), BF16),
        jax.ShapeDtypeStruct((S, B * W), F32),
        jax.ShapeDtypeStruct((B, S, W), BF16),
    )
    row = lambda n: pl.BlockSpec((None, tm, n), lambda b, t: (b, t, 0))
    return pl.pallas_call(
        _inproj_kernel,
        out_shape=out_shape,
        grid=(B, S // tm),
        in_specs=[
            row(D),
            _const_spec((1, D)),
            _const_spec((D, N_NAT)),
            _const_spec((N_T, D)),
            pl.BlockSpec((tm, W), lambda b, t: (t, 0)),
            pl.BlockSpec((tm, W), lambda b, t: (t, 0)),
            pl.BlockSpec((W, tm), lambda b, t: (0, t)),
            pl.BlockSpec((W, tm), lambda b, t: (0, t)),
            _const_spec((LRU_CONV, W)),
            _const_spec((1, W)),
            _const_spec((W, W)),
            _const_spec((1, W)),
            _const_spec((W, W)),
            _const_spec((1, W)),
            _const_spec((1, W)),
        ],
        out_specs=(
            row(N_GLA),
            row(W),
            pl.BlockSpec((None, W, tm), lambda b, t: (b, 0, t)),
            pl.BlockSpec((None, nsub, W, T_ATT), lambda b, t: (b, t, 0, 0)),
            pl.BlockSpec((tm, W), lambda b, t: (t, b)),
            row(W),
        ),
        scratch_shapes=[pltpu.VMEM((SUBLANES, W), F32), pltpu.VMEM((1, W), F32)],
        compiler_params=_params("parallel", "arbitrary"),
        name="inproj",
    )(x, g, wn, wt, cn, sn, ct, st, cw, cb, wa, ba, wx, bx, lam)


def _gla_kernel(p_ref, wg_ref, bg_ref, tril_ref, gm_ref, gain_ref, o_ref, st_ref):
    NB, Lt, _ = p_ref.shape
    C = GLA_CHUNK
    HK = GLA_HEADS * GLA_DK
    HV = GLA_HEADS * GLA_DV

    @pl.when(pl.program_id(1) == 0)
    def _():
        st_ref[...] = jnp.zeros_like(st_ref)

    lane_k = lax.broadcasted_iota(jnp.int32, (C, HK), 1) // GLA_DK
    r_a = lax.broadcasted_iota(jnp.int32, (GLA_HEADS * C, C), 0) % C
    c_a = lax.broadcasted_iota(jnp.int32, (GLA_HEADS * C, C), 1)
    causal = r_a >= c_a
    lane_v = lax.broadcasted_iota(jnp.int32, (C, HV), 1) // GLA_DV
    st_row = lax.broadcasted_iota(jnp.int32, (HV, HK), 0) // GLA_DV
    st_col = lax.broadcasted_iota(jnp.int32, (HV, HK), 1) // GLA_DK
    same_head = st_row == st_col
    tril = tril_ref[...]

    q_dec, k_dec, k_raw, v, bcum = [], [], [], [], []
    for b in range(NB):
        k = p_ref[b, :, HK:2 * HK]
        glr = p_ref[b, :, 2 * HK + 2 * HV:].astype(BF16)
        g = -_softplus(-(_dot(glr, wg_ref[...]) + bg_ref[...])) * (1.0 / GLA_GATE_NORM)
        g_hi, g_lo = _split2(g)
        bc = _dot(tril, g_hi) + _dot(tril, g_lo)
        q_dec.append((p_ref[b, :, 0:HK] * (GLA_DK ** -0.5) * jnp.exp(bc)).astype(BF16))
        k_dec.append((k * jnp.exp(-bc)).astype(BF16))
        k_raw.append(k)
        v.append(p_ref[b, :, 2 * HK:2 * HK + HV].astype(BF16))
        bcum.append(bc)

    outs = [[] for _ in range(NB)]
    for c in range(Lt // C):
        sl = slice(c * C, (c + 1) * C)
        for b in range(NB):
            qd, kd, vc = q_dec[b][sl], k_dec[b][sl], v[b][sl]
            bc = bcum[b][sl]
            blast = bc[C - 1:C, :]
            k_st = (k_raw[b][sl] * jnp.exp(blast - bc)).astype(BF16)
            q_big = jnp.concatenate(
                [jnp.where(lane_k == h, qd, jnp.zeros_like(qd)) for h in range(GLA_HEADS)], axis=0)
            attn = jnp.where(causal, _dot_nt(q_big, kd), 0.0).astype(BF16)
            o_all = _dot(attn, vc)
            o_c = jnp.zeros((C, HV), F32)
            for h in range(GLA_HEADS):
                o_c = o_c + jnp.where(lane_v == h, o_all[h * C:(h + 1) * C], 0.0)
            state = st_ref[b]
            o_c = o_c + _dot_nt(qd, state.astype(BF16))
            kv_t = jnp.where(same_head, _dot_tn(vc, k_st), 0.0)
            st_ref[b] = state * jnp.exp(blast) + kv_t
            outs[b].append(o_c)
    for b in range(NB):
        o = jnp.concatenate(outs[b], axis=0)
        sq_hi, sq_lo = _split2(o * o)
        ms = _dot(sq_hi, gm_ref[...]) + _dot(sq_lo, gm_ref[...])
        o = o * lax.rsqrt(ms + EPS) * gain_ref[...]
        og = p_ref[b, :, 2 * HK + HV:2 * HK + 2 * HV]
        o_ref[b] = (o * (og * _sigmoid(og))).astype(BF16)


def _gla(p, wg, bg, tril, gm, gain):
    B, S, _ = p.shape
    Lt = LT_GLA
    NB = NB_GLA
    HV = GLA_HEADS * GLA_DV
    HK = GLA_HEADS * GLA_DK
    return pl.pallas_call(
        _gla_kernel,
        out_shape=jax.ShapeDtypeStruct((B, S, HV), BF16),
        grid=(B // NB, S // Lt),
        in_specs=[
            pl.BlockSpec((NB, Lt, N_GLA), lambda b, t: (b, t, 0)),
            _const_spec((HK, HK)),
            _const_spec((1, HK)),
            _const_spec((Lt, Lt)),
            _const_spec((HV, HV)),
            _const_spec((1, HV)),
        ],
        out_specs=pl.BlockSpec((NB, Lt, HV), lambda b, t: (b, t, 0)),
        scratch_shapes=[pltpu.VMEM((NB, HV, HK), F32)],
        compiler_params=_params("parallel", "arbitrary"),
        name="gla",
    )(p, wg, bg, tril, gm, gain)


V_AUG = DIFF_DV + 16

def _diff_kernel(qt_ref, k_ref, vt_ref, lq_ref, lk_ref, gain_ref, o_ref, wq_ref, acc_ref, m_ref, s_ref,
                 mx_ref, *, lam_init):
    T = T_ATT
    NB = 2 * DIFF_HEADS
    i = pl.program_id(1)
    qt = qt_ref[...]
    row = lax.broadcasted_iota(jnp.int32, qt.shape, 0) // DIFF_DH
    for n in range(NB):
        wq_ref[n // 2, :, (n % 2) * T:(n % 2 + 1) * T] = jnp.where(row == n, qt, jnp.zeros_like(qt))
    acc_ref[...] = jnp.zeros_like(acc_ref)
    m_ref[...] = jnp.full(m_ref.shape, -1e30, F32)
    ones = jnp.ones((V_AUG - DIFF_DV, T), BF16)

    def produce(j, h, buf, diagonal):
        kb = k_ref[pl.ds(pl.multiple_of(j * T, T), T), :]
        s = _dot(kb, wq_ref[h])
        if diagonal:
            kk = lax.broadcasted_iota(jnp.int32, s.shape, 0)
            qq = lax.broadcasted_iota(jnp.int32, s.shape, 1)
            s = jnp.where(kk <= jnp.where(qq >= T, qq - T, qq), s, -1e30)
        for c in range(2):
            sc = s[:, c * T:(c + 1) * T]
            s_ref[buf, 2 * h + c] = sc
            mx_ref[buf, 2 * h + c] = jnp.max(sc, axis=0, keepdims=True)

    def block(j, j_next, cons, prod):
        for h in range(DIFF_HEADS):
            produce(j_next, h, prod, False)
            va = jnp.concatenate([vt_ref[j, h * DIFF_DV:(h + 1) * DIFF_DV, :], ones], axis=0)
            for c in range(2):
                n = 2 * h + c
                m = m_ref[n]
                m_new = jnp.maximum(m, mx_ref[cons, n])
                p = jnp.exp2(s_ref[cons, n] - m_new).astype(BF16)
                acc_ref[n] = jnp.exp2(m - m_new) * acc_ref[n] + _dot(va, p)
                m_ref[n] = m_new

    for h in range(DIFF_HEADS):
        produce(i, h, 0, True)
    block(i, 0, 0, 1)

    def body(t, carry):
        block(2 * t, 2 * t + 1, 1, 0)
        block(2 * t + 1, jnp.minimum(2 * t + 2, i - 1), 0, 1)
        return carry

    lax.fori_loop(0, i >> 1, body, 0)

    @pl.when((i & 1) == 1)
    def _():
        block(i - 1, i - 1, 1, 0)

    e = jnp.exp(jnp.sum(lq_ref[...] * lk_ref[...], axis=-1, keepdims=True))
    lam = e[0:1] - e[1:2] + lam_init
    heads = []
    for h in range(DIFF_HEADS):
        a1, a2 = acc_ref[2 * h], acc_ref[2 * h + 1]
        o1 = a1[:DIFF_DV] / a1[DIFF_DV:DIFF_DV + 1]
        o2 = a2[:DIFF_DV] / a2[DIFF_DV:DIFF_DV + 1]
        oh = o1 - lam * o2
        ms = jnp.mean(oh * oh, axis=0, keepdims=True)
        heads.append(oh * lax.rsqrt(ms + EPS))
    ot = jnp.concatenate(heads, axis=0) * gain_ref[...] * (1.0 - lam_init)
    o_ref[...] = ot.T.astype(BF16)


def _diff(qt, k, vt, lq, lk, gain_t, lam_init):
    B, W, S = qt.shape
    T = T_ATT
    return pl.pallas_call(
        functools.partial(_diff_kernel, lam_init=lam_init),
        out_shape=jax.ShapeDtypeStruct((B, S, W), BF16),
        grid=(B, S // T),
        in_specs=[
            pl.BlockSpec((None, W, T), lambda b, i: (b, 0, i)),
            pl.BlockSpec((None, S, W), lambda b, i: (b, 0, 0)),
            pl.BlockSpec((None, S // T, W, T), lambda b, i: (b, 0, 0, 0)),
            _const_spec((2, DIFF_DH)),
            _const_spec((2, DIFF_DH)),
            _const_spec((W, T)),
        ],
        out_specs=pl.BlockSpec((None, T, W), lambda b, i: (b, i, 0)),
        scratch_shapes=[pltpu.VMEM((DIFF_HEADS, W, 2 * T), BF16),
                        pltpu.VMEM((2 * DIFF_HEADS, V_AUG, T), F32),
                        pltpu.VMEM((2 * DIFF_HEADS, 1, T), F32),
                        pltpu.VMEM((2, 2 * DIFF_HEADS, T, T), F32),
                        pltpu.VMEM((2, 2 * DIFF_HEADS, 1, T), F32)],
        compiler_params=_params("parallel", "arbitrary"),
        name="diff",
    )(qt, k, vt, lq, lk, gain_t)


def _s5_prep_kernel(ls_ref, lr_ref, li_ref, br_ref, bi_ref,
                    bbr_ref, bbi_ref, ab_ref):
    step = jnp.exp(ls_ref[...])
    lr, li = lr_ref[...], li_ref[...]
    mag = jnp.exp(lr * step)
    ab_re = mag * jnp.cos(li * step)
    ab_im = mag * jnp.sin(li * step)
    den = lr * lr + li * li
    nr, ni = ab_re - 1.0, ab_im
    cr = (nr * lr + ni * li) / den
    ci = (ni * lr - nr * li) / den
    bbr_ref[...] = cr * br_ref[...] - ci * bi_ref[...]
    bbi_ref[...] = cr * bi_ref[...] + ci * br_ref[...]

    ab_ref[0] = jnp.broadcast_to(ab_re, (SUBLANES, S5_LANES))
    ab_ref[1] = jnp.broadcast_to(ab_im, (SUBLANES, S5_LANES))


def _s5_prep(ls_e, lr, li, br_e, bi_e):
    L = ls_e.shape[0]
    N = S5_LANES
    vec = pl.BlockSpec((None, 1, N), lambda l: (l, 0, 0))
    mat = pl.BlockSpec((None, S5_CH, N), lambda l: (l, 0, 0))
    return pl.pallas_call(
        _s5_prep_kernel,
        out_shape=(
            jax.ShapeDtypeStruct((L, S5_CH, N), F32),
            jax.ShapeDtypeStruct((L, S5_CH, N), F32),
            jax.ShapeDtypeStruct((L, 2, SUBLANES, N), F32),
        ),
        grid=(L,),
        in_specs=[vec, vec, vec, mat, mat],
        out_specs=(mat, mat, pl.BlockSpec((None, 2, SUBLANES, N), lambda l: (l, 0, 0, 0))),
        compiler_params=_params("arbitrary"),
        name="s5_prep",
    )(ls_e, lr, li, br_e, bi_e)


def _s5_kernel(u_ref, perm_ref, permt_ref, bb_ref, cm_ref, ab_ref, d_ref, wglu_ref, bglu_ref,
               o_ref, z_ref, st_ref):
    N = S5_LANES
    W = GROUP_WIDTH
    TS = TS_S5
    nsub = u_ref.shape[0] // TS

    @pl.when(pl.program_id(0) == 0)
    def _():
        st_ref[...] = jnp.zeros_like(st_ref)

    def slab(k):
        return jnp.concatenate(
            [u_ref[k * TS:(k + 1) * TS, b * W:(b + 1) * W] for b in range(SUBLANES)], axis=0)

    def project(k):
        u_tb = _dot(perm_ref[...], slab(k).astype(BF16)).astype(BF16)
        z_ref[k] = _dot(u_tb, bb_ref[...])

    ar, ai = ab_ref[0], ab_ref[1]
    project(0)
    if nsub > 1:
        project(1)
    for k in range(nsub):
        zk = z_ref.at[k]
        zr, zi = st_ref[0], st_ref[1]
        for t in range(TS):
            rs = slice(t * SUBLANES, (t + 1) * SUBLANES)
            zr, zi = (ar * zr - ai * zi + zk[rs, :N], ar * zi + ai * zr + zk[rs, N:])
            zk[rs, :N] = zr
            zk[rs, N:] = zi
        st_ref[0] = zr
        st_ref[1] = zi

        half = SUBLANES * TS // 2
        y_tb = jnp.concatenate([_dot(zk[:half, :].astype(BF16), cm_ref[...]),
                                _dot(zk[half:, :].astype(BF16), cm_ref[...])], axis=0)
        if k + 2 < nsub:
            project(k + 2)
        y_hi, y_lo = _split2(y_tb)
        y = _dot(permt_ref[...], y_hi) + _dot(permt_ref[...], y_lo)
        y = _gelu_tanh(y + d_ref[...] * slab(k))
        gate = _sigmoid(_dot(y.astype(BF16), wglu_ref[...]) + bglu_ref[...])
        out = (y * gate).astype(BF16)
        for b in range(SUBLANES):
            o_ref[k * TS:(k + 1) * TS, b * W:(b + 1) * W] = out[b * TS:(b + 1) * TS]


def _s5(u, perm, permt, bb, cm, ab, d, wglu, bglu):
    S, BW = u.shape
    W = GROUP_WIDTH
    assert BW == SUBLANES * W, "the S5 kernel maps the batch onto the 8 sublanes"
    rows = SUBLANES * TS_S5
    Lt = TS_S5 * NSUB_S5
    N = S5_LANES
    return pl.pallas_call(
        _s5_kernel,
        out_shape=jax.ShapeDtypeStruct((S, BW), BF16),
        grid=(S // Lt,),
        in_specs=[
            pl.BlockSpec((Lt, BW), lambda t: (t, 0)),
            _const_spec((rows, rows)),
            _const_spec((rows, rows)),
            _const_spec((W, 2 * N)),
            _const_spec((2 * N, W)),
            _const_spec((2, SUBLANES, N)),
            _const_spec((1, W)),
            _const_spec((W, W)),
            _const_spec((1, W)),
        ],
        out_specs=pl.BlockSpec((Lt, BW), lambda t: (t, 0)),
        scratch_shapes=[pltpu.VMEM((NSUB_S5, rows, 2 * N), F32), pltpu.VMEM((2, SUBLANES, N), F32)],
        compiler_params=_params("arbitrary"),
        name="s5",
    )(u, perm, permt, bb, cm, ab, d, wglu, bglu)


def _outffn_kernel(x_ref, oa_ref, ob_ref, oc_ref, od_ref, wo_ref, g1_ref, g2_ref, g3_ref,
                   w1_ref, w2_ref, out_ref):
    mixin = jnp.concatenate([oa_ref[...], ob_ref[...], oc_ref[...], od_ref[...]], axis=-1)
    x1 = x_ref[...] + _rms(_dot(mixin, wo_ref[...]), g1_ref[...])
    h = _rms(x1, g2_ref[...]).astype(BF16)
    f = jnp.zeros_like(x1)
    for c in range(D_FF // FF_CHUNK):
        a = jnp.maximum(_dot(h, w1_ref[:, c * FF_CHUNK:(c + 1) * FF_CHUNK]), 0.0)
        f = f + _dot((a * a).astype(BF16), w2_ref[c * FF_CHUNK:(c + 1) * FF_CHUNK, :])
    out_ref[...] = x1 + _rms(f, g3_ref[...])


def _outffn(x, oa, ob, oc, od, wo, g1, g2, g3, w1, w2):
    B, S, D = x.shape
    tm = TM_FFN
    W = GROUP_WIDTH
    row = lambda n: pl.BlockSpec((None, tm, n), lambda b, t: (b, t, 0))
    once = lambda shape: pl.BlockSpec(shape, lambda b, t: (0, 0), pipeline_mode=pl.Buffered(1))
    return pl.pallas_call(
        _outffn_kernel,
        out_shape=jax.ShapeDtypeStruct((B, S, D), F32),
        grid=(B, S // tm),
        in_specs=[row(D), row(W), row(W), pl.BlockSpec((tm, W), lambda b, t: (t, b)), row(W),
                  once((D, D)), _const_spec((1, D)), _const_spec((1, D)), _const_spec((1, D)),
                  once((D, D_FF)), once((D_FF, D))],
        out_specs=row(D),
        compiler_params=_params("parallel", "parallel"),
        name="outffn",
    )(x, oa, ob, oc, od, wo, g1, g2, g3, w1, w2)


def _rope_tables(S):
    inv = ROPE_THETA ** (-jnp.arange(0, DIFF_DH, 2, dtype=F32) / DIFF_DH)
    ang = jnp.arange(S, dtype=F32)[:, None] * inv[None, :]
    cos = jnp.concatenate([jnp.cos(ang), jnp.cos(ang)], axis=-1)
    sin = jnp.concatenate([-jnp.sin(ang), jnp.sin(ang)], axis=-1)
    reps = GROUP_WIDTH // DIFF_DH
    return jnp.tile(cos, (1, reps)), jnp.tile(sin, (1, reps))


def _block_diag(blocks):
    n, a, b = blocks.shape
    eye = jnp.eye(n, dtype=blocks.dtype)
    return (eye[:, None, :, None] * blocks[:, :, None, :]).reshape(n * a, n * b)


def _s5_perm():
    n = SUBLANES * TS_S5
    r = np.arange(n)
    src = (r % SUBLANES) * TS_S5 + r // SUBLANES
    p = np.zeros((n, n), np.float32)
    p[r, src] = 1.0
    return p


def kernel(x, ln_mix_pre, ln_mix_post, ln_ffn_pre, ln_ffn_post, w_in, w_out, gla_w_gate, gla_b_gate, gla_norm, diff_lq1, diff_lk1, diff_lq2, diff_lk2, diff_norm, s5_log_step, s5_a_re, s5_a_im, s5_b_re, s5_b_im, s5_c_re, s5_c_im, s5_d, s5_w_glu, s5_b_glu, lru_conv_w, lru_conv_b, lru_w_a, lru_b_a, lru_w_x, lru_b_x, lru_lambda, ffn_w1, ffn_w2):
    B, S, D = x.shape
    L = w_in.shape[0]
    W = GROUP_WIDTH
    HK = GLA_HEADS * GLA_DK

    cos_n, sin_n = _rope_tables(S)
    q_scale = DIFF_DH ** -0.5 * math.log2(math.e)
    cos_t = (cos_n * q_scale).T
    sin_t = (sin_n * q_scale).T
    tril = jnp.asarray(np.kron(np.eye(LT_GLA // GLA_CHUNK), np.tril(np.ones((GLA_CHUNK, GLA_CHUNK)))), BF16)
    gmean = jnp.asarray(np.kron(np.eye(GLA_HEADS), np.full((GLA_DV, GLA_DV), 1.0 / GLA_DV)), BF16)
    perm = _s5_perm()
    perm_b, permt_b = jnp.asarray(perm, BF16), jnp.asarray(perm.T, BF16)

    ls_e = jnp.repeat(s5_log_step, S5_STATE, axis=1)[:, None, :]
    lr_e = s5_a_re.reshape(L, 1, S5_LANES)
    li_e = s5_a_im.reshape(L, 1, S5_LANES)
    br_e = s5_b_re.transpose(0, 3, 1, 2).reshape(L, S5_CH, S5_LANES)
    bi_e = s5_b_im.transpose(0, 3, 1, 2).reshape(L, S5_CH, S5_LANES)
    bbr, bbi, ab = _s5_prep(ls_e, lr_e, li_e, br_e, bi_e)
    grp = jnp.arange(S5_LANES) // S5_STATE
    gsel = (jnp.arange(S5_GROUPS)[:, None] == grp[None, :]).astype(F32)

    for l in range(L):
        lam_init = 0.8 - 0.6 * math.exp(-0.3 * l)
        wi = w_in[l]
        offs = np.cumsum([0, HK, HK, 2 * HK, GLA_RANK, 2 * HK, W, W, W, W, W, W])
        seg = [wi[:, offs[i]:offs[i + 1]] for i in range(11)]
        g_q, g_k, g_v, g_lr, g_og, d_q, d_k, d_v, s_u, r_x, r_g = seg
        glr_pad = jnp.pad(g_lr, ((0, 0), (0, LANES - GLA_RANK)))
        wn = jnp.concatenate([g_q, g_k, g_v, g_og, glr_pad, d_k, s_u, r_x, r_g], axis=1).astype(BF16)
        wt = jnp.concatenate([d_q, d_v], axis=1).T.astype(BF16)

        gla_p, kd, qt, vt, su, o_d = _inproj(
            x, ln_mix_pre[l][None], wn, wt, cos_n, sin_n, cos_t, sin_t,
            lru_conv_w[l], lru_conv_b[l][None],
            _block_diag(lru_w_a[l]).astype(BF16), lru_b_a[l][None],
            _block_diag(lru_w_x[l]).astype(BF16), lru_b_x[l][None], lru_lambda[l][None])

        wg = jnp.pad(gla_w_gate[l], ((0, LANES - GLA_RANK), (0, 0))).astype(BF16)
        o_a = _gla(gla_p, wg, gla_b_gate[l][None], tril, gmean,
                   jnp.tile(gla_norm[l], GLA_HEADS)[None])

        lq = jnp.stack([diff_lq1[l], diff_lq2[l]])
        lk = jnp.stack([diff_lk1[l], diff_lk2[l]])
        gain_t = jnp.broadcast_to(jnp.tile(diff_norm[l], DIFF_HEADS)[:, None], (W, T_ATT))
        o_b = _diff(qt, kd, vt, lq, lk, gain_t, lam_init)

        bb = jnp.concatenate(
            [(gsel[:, None, :] * b[None, :, :]).reshape(W, S5_LANES) for b in (bbr[l], bbi[l])],
            axis=1).astype(BF16)
        cm = jnp.concatenate([_block_diag(s5_c_re[l]).T, -_block_diag(s5_c_im[l]).T], axis=0).astype(BF16)
        o_c = _s5(su, perm_b, permt_b, bb, cm, ab[l], s5_d[l][None],
                  s5_w_glu[l].astype(BF16), s5_b_glu[l][None])

        x = _outffn(x, o_a, o_b, o_c, o_d, w_out[l].astype(BF16),
                    ln_mix_post[l][None], ln_ffn_pre[l][None], ln_ffn_post[l][None],
                    ffn_w1[l].astype(BF16), ffn_w2[l].astype(BF16))
    return x
```

```python
import functools
import math

import numpy as np
import jax
import jax.numpy as jnp
from jax import lax
from jax.experimental import pallas as pl
from jax.experimental.pallas import tpu as pltpu

F32 = jnp.float32
BF16 = jnp.bfloat16

D_MODEL = 1024
GROUP_WIDTH = 256
EPS = 1e-6
GLA_HEADS, GLA_DK, GLA_DV, GLA_RANK, GLA_CHUNK = 4, 32, 64, 16, 64
GLA_GATE_NORM = 16.0
DIFF_HEADS, DIFF_DH, DIFF_DV = 4, 32, 64
ROPE_THETA = 10000.0
S5_GROUPS, S5_STATE, S5_CH = 16, 64, 16
S5_LANES = S5_GROUPS * S5_STATE
LRU_BLOCKS, LRU_BLOCK, LRU_CONV, LRU_C = 4, 64, 4, 8.0
D_FF = 4 * D_MODEL

SUBLANES = 8
LANES = 128
VMEM_LIMIT = 56 * 1024 * 1024

TM_INPROJ = 512
LRU_SCAN_ROWS = 64
T_ATT = 512
LT_GLA = 256
NB_GLA = 4
TS_S5 = 32
NSUB_S5 = 8
TM_FFN = 512
FF_CHUNK = 1024


def _dot(a, b):
    return jnp.dot(a, b, preferred_element_type=F32)


def _dot_nt(a, b):
    return lax.dot_general(a, b, (((1,), (1,)), ((), ())), preferred_element_type=F32)


def _dot_tn(a, b):
    return lax.dot_general(a, b, (((0,), (0,)), ((), ())), preferred_element_type=F32)


def _split2(x):
    hi = x.astype(BF16)
    lo = (x - hi.astype(F32)).astype(BF16)
    return hi, lo


def _split3(x):
    hi = x.astype(BF16)
    r = x - hi.astype(F32)
    mid = r.astype(BF16)
    lo = (r - mid.astype(F32)).astype(BF16)
    return hi, mid, lo


def _sigmoid(x):
    return 1.0 / (1.0 + jnp.exp(-x))


def _softplus(x):
    return jnp.maximum(x, 0.0) + jnp.log1p(jnp.exp(-jnp.abs(x)))


def _gelu_tanh(x):
    c = math.sqrt(2.0 / math.pi)
    return 0.5 * x * (1.0 + jnp.tanh(c * (x + 0.044715 * (x * x * x))))


def _rms(x, g):
    ms = jnp.mean(x * x, axis=-1, keepdims=True)
    return x * lax.rsqrt(ms + EPS) * g


def _params(*sem):
    return pltpu.CompilerParams(dimension_semantics=sem, vmem_limit_bytes=VMEM_LIMIT)


def _const_spec(shape):
    nd = len(shape)
    return pl.BlockSpec(shape, lambda *_: (0,) * nd)


N_GLA = 896
N_NAT = N_GLA + 4 * GROUP_WIDTH
N_T = 2 * GROUP_WIDTH


def _shift_rows(x, d, fill):
    n = x.shape[0]
    if d % SUBLANES == 0:
        return jnp.concatenate([jnp.full((d, x.shape[1]), fill, x.dtype), x[:n - d]], axis=0)
    r = lax.broadcasted_iota(jnp.int32, x.shape, 0)
    return jnp.where(r >= d, pltpu.roll(x, d, 0), fill)


def _inproj_kernel(x_ref, g_ref, wn_ref, wt_ref, cn_ref, sn_ref, ct_ref, st_ref,
                   cw_ref, cb_ref, wa_ref, ba_ref, wx_ref, bx_ref, lam_ref,
                   gla_ref, kd_ref, qt_ref, vt_ref, su_ref, od_ref, tail_ref, hst_ref):
    tm = x_ref.shape[0]
    W = GROUP_WIDTH
    HALF = DIFF_DH // 2

    @pl.when(pl.program_id(1) == 0)
    def _():
        tail_ref[...] = jnp.zeros_like(tail_ref)
        hst_ref[...] = jnp.zeros_like(hst_ref)

    h = _rms(x_ref[...], g_ref[...]).astype(BF16)
    o = N_GLA
    rxg = _dot(h, wn_ref[:, o + 2 * W:o + 4 * W])
    xb, gate = rxg[:, :W], rxg[:, W:]
    ext = jnp.concatenate([tail_ref[...], xb], axis=0)
    tail_ref[...] = xb[tm - SUBLANES:]
    xc = xb * cw_ref[LRU_CONV - 1:LRU_CONV, :] + cb_ref[...]
    for sh in range(1, LRU_CONV):
        xs = pltpu.roll(ext, sh, 0)[SUBLANES:]
        xc = xc + xs * cw_ref[LRU_CONV - 1 - sh:LRU_CONV - sh, :]
    xcb = xc.astype(BF16)

    k = _dot(h, wn_ref[:, o:o + W])
    first_half = lax.broadcasted_iota(jnp.int32, k.shape, 1) % DIFF_DH < HALF
    k_sw = jnp.where(first_half, pltpu.roll(k, W - HALF, 1), pltpu.roll(k, HALF, 1))
    kd_ref[...] = (k * cn_ref[...] + k_sw * sn_ref[...]).astype(BF16)

    r = _sigmoid(_dot(xcb, wa_ref[...]) + ba_ref[...])
    ig = _sigmoid(_dot(xcb, wx_ref[...]) + bx_ref[...])

    gla_ref[...] = _dot(h, wn_ref[:, :N_GLA])
    su_ref[...] = _dot(h, wn_ref[:, o + W:o + 2 * W])
    q = _dot_nt(wt_ref[:W, :], h)
    q_sw = jnp.concatenate(
        [q[g * DIFF_DH + off:g * DIFF_DH + off + HALF] for g in range(W // DIFF_DH) for off in (HALF, 0)],
        axis=0)
    qt_ref[...] = (q * ct_ref[...] + q_sw * st_ref[...]).astype(BF16)
    vt = _dot_nt(wt_ref[W:, :], h).astype(BF16)
    for j in range(vt_ref.shape[0]):
        vt_ref[j] = vt[:, j * T_ATT:(j + 1) * T_ATT]

    sp = _softplus(-lam_ref[...])
    h_prev = hst_ref[...]
    for s0 in range(0, tm, LRU_SCAN_ROWS):
        rows = slice(s0, s0 + LRU_SCAN_ROWS)
        log_a = -LRU_C * r[rows] * sp
        a = jnp.exp(log_a)
        th = jnp.tanh(log_a)
        mult = jnp.sqrt(jnp.maximum(-2.0 * th / (1.0 - th), 1e-12))
        b = mult * (ig[rows] * xc[rows])
        d = 1
        while d < LRU_SCAN_ROWS:
            b = b + a * _shift_rows(b, d, 0.0)
            a = a * _shift_rows(a, d, 1.0)
            d *= 2
        hs = b + a * h_prev
        h_prev = hs[LRU_SCAN_ROWS - 1:LRU_SCAN_ROWS]
        od_ref[rows, :] = (hs * _gelu_tanh(gate[rows])).astype(BF16)
    hst_ref[...] = h_prev


def _inproj(x, g, wn, wt, cn, sn, ct, st, cw, cb, wa, ba, wx, bx, lam):
    B, S, D = x.shape
    tm = TM_INPROJ
    W = GROUP_WIDTH
    nsub = tm // T_ATT
    out_shape = (
        jax.ShapeDtypeStruct((B, S, N_GLA), F32),
        jax.ShapeDtypeStruct((B, S, W), BF16),
        jax.ShapeDtypeStruct((B, W, S), BF16),
        jax.ShapeDtypeStruct((B, S // T_ATT, W, T_ATT), BF16),
        jax.ShapeDtypeStruct((S, B * W), F32),
        jax.ShapeDtypeStruct((B, S, W), BF16),
    )
    row = lambda n: pl.BlockSpec((None, tm, n), lambda b, t: (b, t, 0))
    return pl.pallas_call(
        _inproj_kernel,
        out_shape=out_shape,
        grid=(B, S // tm),
        in_specs=[
            row(D),
            _const_spec((1, D)),
            _const_spec((D, N_NAT)),
            _const_spec((N_T, D)),
            pl.BlockSpec((tm, W), lambda b, t: (t, 0)),
            pl.BlockSpec((tm, W), lambda b, t: (t, 0)),
            pl.BlockSpec((W, tm), lambda b, t: (0, t)),
            pl.BlockSpec((W, tm), lambda b, t: (0, t)),
            _const_spec((LRU_CONV, W)),
            _const_spec((1, W)),
            _const_spec((W, W)),
            _const_spec((1, W)),
            _const_spec((W, W)),
            _const_spec((1, W)),
            _const_spec((1, W)),
        ],
        out_specs=(
            row(N_GLA),
            row(W),
            pl.BlockSpec((None, W, tm), lambda b, t: (b, 0, t)),
            pl.BlockSpec((None, nsub, W, T_ATT), lambda b, t: (b, t, 0, 0)),
            pl.BlockSpec((tm, W), lambda b, t: (t, b)),
            row(W),
        ),
        scratch_shapes=[pltpu.VMEM((SUBLANES, W), F32), pltpu.VMEM((1, W), F32)],
        compiler_params=_params("parallel", "arbitrary"),
        name="inproj",
    )(x, g, wn, wt, cn, sn, ct, st, cw, cb, wa, ba, wx, bx, lam)


def _gla_kernel(p_ref, wg_ref, bg_ref, tril_ref, gm_ref, gain_ref, o_ref, st_ref):
    NB, Lt, _ = p_ref.shape
    C = GLA_CHUNK
    HK = GLA_HEADS * GLA_DK
    HV = GLA_HEADS * GLA_DV

    @pl.when(pl.program_id(1) == 0)
    def _():
        st_ref[...] = jnp.zeros_like(st_ref)

    lane_k = lax.broadcasted_iota(jnp.int32, (C, HK), 1) // GLA_DK
    r_a = lax.broadcasted_iota(jnp.int32, (GLA_HEADS * C, C), 0) % C
    c_a = lax.broadcasted_iota(jnp.int32, (GLA_HEADS * C, C), 1)
    causal = r_a >= c_a
    lane_v = lax.broadcasted_iota(jnp.int32, (C, HV), 1) // GLA_DV
    st_row = lax.broadcasted_iota(jnp.int32, (HV, HK), 0) // GLA_DV
    st_col = lax.broadcasted_iota(jnp.int32, (HV, HK), 1) // GLA_DK
    same_head = st_row == st_col
    tril = tril_ref[...]

    q_dec, k_dec, k_raw, v, bcum = [], [], [], [], []
    for b in range(NB):
        k = p_ref[b, :, HK:2 * HK]
        glr = p_ref[b, :, 2 * HK + 2 * HV:].astype(BF16)
        g = -_softplus(-(_dot(glr, wg_ref[...]) + bg_ref[...])) * (1.0 / GLA_GATE_NORM)
        g_hi, g_lo = _split2(g)
        bc = _dot(tril, g_hi) + _dot(tril, g_lo)
        q_dec.append((p_ref[b, :, 0:HK] * (GLA_DK ** -0.5) * jnp.exp(bc)).astype(BF16))
        k_dec.append((k * jnp.exp(-bc)).astype(BF16))
        k_raw.append(k)
        v.append(p_ref[b, :, 2 * HK:2 * HK + HV].astype(BF16))
        bcum.append(bc)

    outs = [[] for _ in range(NB)]
    rng = range(NB)
    for c in range(Lt // C):
        sl = slice(c * C, (c + 1) * C)
        blast = [bcum[b][sl][C - 1:C, :] for b in rng]
        scores = []
        for b in rng:
            qd = q_dec[b][sl]
            q_big = jnp.concatenate(
                [jnp.where(lane_k == h, qd, jnp.zeros_like(qd)) for h in range(GLA_HEADS)], axis=0)
            scores.append(_dot_nt(q_big, k_dec[b][sl]))
        state = [st_ref[b] for b in rng]
        o_inter = [_dot_nt(q_dec[b][sl], state[b].astype(BF16)) for b in rng]
        kv_t = []
        for b in rng:
            k_st = (k_raw[b][sl] * jnp.exp(blast[b] - bcum[b][sl])).astype(BF16)
            kv_t.append(_dot_tn(v[b][sl], k_st))
        o_all = [_dot(jnp.where(causal, scores[b], 0.0).astype(BF16), v[b][sl]) for b in rng]
        for b in rng:
            o_c = o_inter[b]
            for h in range(GLA_HEADS):
                o_c = o_c + jnp.where(lane_v == h, o_all[b][h * C:(h + 1) * C], 0.0)
            st_ref[b] = state[b] * jnp.exp(blast[b]) + jnp.where(same_head, kv_t[b], 0.0)
            outs[b].append(o_c)
    for b in range(NB):
        o = jnp.concatenate(outs[b], axis=0)
        sq_hi, sq_lo = _split2(o * o)
        ms = _dot(sq_hi, gm_ref[...]) + _dot(sq_lo, gm_ref[...])
        o = o * lax.rsqrt(ms + EPS) * gain_ref[...]
        og = p_ref[b, :, 2 * HK + HV:2 * HK + 2 * HV]
        o_ref[b] = (o * (og * _sigmoid(og))).astype(BF16)


def _gla(p, wg, bg, tril, gm, gain):
    B, S, _ = p.shape
    Lt = LT_GLA
    NB = NB_GLA
    HV = GLA_HEADS * GLA_DV
    HK = GLA_HEADS * GLA_DK
    return pl.pallas_call(
        _gla_kernel,
        out_shape=jax.ShapeDtypeStruct((B, S, HV), BF16),
        grid=(B // NB, S // Lt),
        in_specs=[
            pl.BlockSpec((NB, Lt, N_GLA), lambda b, t: (b, t, 0)),
            _const_spec((HK, HK)),
            _const_spec((1, HK)),
            _const_spec((Lt, Lt)),
            _const_spec((HV, HV)),
            _const_spec((1, HV)),
        ],
        out_specs=pl.BlockSpec((NB, Lt, HV), lambda b, t: (b, t, 0)),
        scratch_shapes=[pltpu.VMEM((NB, HV, HK), F32)],
        compiler_params=_params("parallel", "arbitrary"),
        name="gla",
    )(p, wg, bg, tril, gm, gain)


V_AUG = DIFF_DV + 16

def _diff_kernel(qt_ref, k_ref, vt_ref, lq_ref, lk_ref, gain_ref, o_ref, wq_ref, acc_ref, m_ref, s_ref,
                 mx_ref, *, lam_init):
    T = T_ATT
    NB = 2 * DIFF_HEADS
    i = pl.program_id(1)
    qt = qt_ref[...]
    row = lax.broadcasted_iota(jnp.int32, qt.shape, 0) // DIFF_DH
    for n in range(NB):
        wq_ref[n // 2, :, (n % 2) * T:(n % 2 + 1) * T] = jnp.where(row == n, qt, jnp.zeros_like(qt))
    acc_ref[...] = jnp.zeros_like(acc_ref)
    m_ref[...] = jnp.full(m_ref.shape, -1e30, F32)
    ones = jnp.ones((V_AUG - DIFF_DV, T), BF16)

    def produce(j, h, buf, diagonal):
        kb = k_ref[pl.ds(pl.multiple_of(j * T, T), T), :]
        s = _dot(kb, wq_ref[h])
        if diagonal:
            kk = lax.broadcasted_iota(jnp.int32, s.shape, 0)
            qq = lax.broadcasted_iota(jnp.int32, s.shape, 1)
            s = jnp.where(kk <= jnp.where(qq >= T, qq - T, qq), s, -1e30)
        for c in range(2):
            sc = s[:, c * T:(c + 1) * T]
            s_ref[buf, 2 * h + c] = sc
            mx_ref[buf, 2 * h + c] = jnp.max(sc, axis=0, keepdims=True)

    def block(j, j_next, cons, prod):
        for h in range(DIFF_HEADS):
            produce(j_next, h, prod, False)
            va = jnp.concatenate([vt_ref[j, h * DIFF_DV:(h + 1) * DIFF_DV, :], ones], axis=0)
            for c in range(2):
                n = 2 * h + c
                m = m_ref[n]
                m_new = jnp.maximum(m, mx_ref[cons, n])
                p = jnp.exp2(s_ref[cons, n] - m_new).astype(BF16)
                acc_ref[n] = jnp.exp2(m - m_new) * acc_ref[n] + _dot(va, p)
                m_ref[n] = m_new

    for h in range(DIFF_HEADS):
        produce(i, h, 0, True)
    block(i, 0, 0, 1)

    def body(t, carry):
        block(2 * t, 2 * t + 1, 1, 0)
        block(2 * t + 1, jnp.minimum(2 * t + 2, i - 1), 0, 1)
        return carry

    lax.fori_loop(0, i >> 1, body, 0)

    @pl.when((i & 1) == 1)
    def _():
        block(i - 1, i - 1, 1, 0)

    e = jnp.exp(jnp.sum(lq_ref[...] * lk_ref[...], axis=-1, keepdims=True))
    lam = e[0:1] - e[1:2] + lam_init
    heads = []
    for h in range(DIFF_HEADS):
        a1, a2 = acc_ref[2 * h], acc_ref[2 * h + 1]
        o1 = a1[:DIFF_DV] / a1[DIFF_DV:DIFF_DV + 1]
        o2 = a2[:DIFF_DV] / a2[DIFF_DV:DIFF_DV + 1]
        oh = o1 - lam * o2
        ms = jnp.mean(oh * oh, axis=0, keepdims=True)
        heads.append(oh * lax.rsqrt(ms + EPS))
    ot = jnp.concatenate(heads, axis=0) * gain_ref[...] * (1.0 - lam_init)
    o_ref[...] = ot.T.astype(BF16)


def _diff(qt, k, vt, lq, lk, gain_t, lam_init):
    B, W, S = qt.shape
    T = T_ATT
    return pl.pallas_call(
        functools.partial(_diff_kernel, lam_init=lam_init),
        out_shape=jax.ShapeDtypeStruct((B, S, W), BF16),
        grid=(B, S // T),
        in_specs=[
            pl.BlockSpec((None, W, T), lambda b, i: (b, 0, i)),
            pl.BlockSpec((None, S, W), lambda b, i: (b, 0, 0)),
            pl.BlockSpec((None, S // T, W, T), lambda b, i: (b, 0, 0, 0)),
            _const_spec((2, DIFF_DH)),
            _const_spec((2, DIFF_DH)),
            _const_spec((W, T)),
        ],
        out_specs=pl.BlockSpec((None, T, W), lambda b, i: (b, i, 0)),
        scratch_shapes=[pltpu.VMEM((DIFF_HEADS, W, 2 * T), BF16),
                        pltpu.VMEM((2 * DIFF_HEADS, V_AUG, T), F32),
                        pltpu.VMEM((2 * DIFF_HEADS, 1, T), F32),
                        pltpu.VMEM((2, 2 * DIFF_HEADS, T, T), F32),
                        pltpu.VMEM((2, 2 * DIFF_HEADS, 1, T), F32)],
        compiler_params=_params("parallel", "arbitrary"),
        name="diff",
    )(qt, k, vt, lq, lk, gain_t)


def _s5_prep_kernel(ls_ref, lr_ref, li_ref, br_ref, bi_ref,
                    bbr_ref, bbi_ref, ab_ref):
    step = jnp.exp(ls_ref[...])
    lr, li = lr_ref[...], li_ref[...]
    mag = jnp.exp(lr * step)
    ab_re = mag * jnp.cos(li * step)
    ab_im = mag * jnp.sin(li * step)
    den = lr * lr + li * li
    nr, ni = ab_re - 1.0, ab_im
    cr = (nr * lr + ni * li) / den
    ci = (ni * lr - nr * li) / den
    bbr_ref[...] = cr * br_ref[...] - ci * bi_ref[...]
    bbi_ref[...] = cr * bi_ref[...] + ci * br_ref[...]

    ab_ref[0] = jnp.broadcast_to(ab_re, (SUBLANES, S5_LANES))
    ab_ref[1] = jnp.broadcast_to(ab_im, (SUBLANES, S5_LANES))


def _s5_prep(ls_e, lr, li, br_e, bi_e):
    L = ls_e.shape[0]
    N = S5_LANES
    vec = pl.BlockSpec((None, 1, N), lambda l: (l, 0, 0))
    mat = pl.BlockSpec((None, S5_CH, N), lambda l: (l, 0, 0))
    return pl.pallas_call(
        _s5_prep_kernel,
        out_shape=(
            jax.ShapeDtypeStruct((L, S5_CH, N), F32),
            jax.ShapeDtypeStruct((L, S5_CH, N), F32),
            jax.ShapeDtypeStruct((L, 2, SUBLANES, N), F32),
        ),
        grid=(L,),
        in_specs=[vec, vec, vec, mat, mat],
        out_specs=(mat, mat, pl.BlockSpec((None, 2, SUBLANES, N), lambda l: (l, 0, 0, 0))),
        compiler_params=_params("arbitrary"),
        name="s5_prep",
    )(ls_e, lr, li, br_e, bi_e)


def _s5_kernel(u_ref, perm_ref, permt_ref, bb_ref, cm_ref, ab_ref, d_ref, wglu_ref, bglu_ref,
               o_ref, z_ref, st_ref):
    N = S5_LANES
    W = GROUP_WIDTH
    TS = TS_S5
    nsub = u_ref.shape[0] // TS

    @pl.when(pl.program_id(0) == 0)
    def _():
        st_ref[...] = jnp.zeros_like(st_ref)

    def slab(k):
        return jnp.concatenate(
            [u_ref[k * TS:(k + 1) * TS, b * W:(b + 1) * W] for b in range(SUBLANES)], axis=0)

    def project(k):
        u_tb = _dot(perm_ref[...], slab(k).astype(BF16)).astype(BF16)
        z_ref[k] = _dot(u_tb, bb_ref[...])

    def glu_store(k, y):
        gate = _sigmoid(_dot(y.astype(BF16), wglu_ref[...]) + bglu_ref[...])
        out = (y * gate).astype(BF16)
        for b in range(SUBLANES):
            o_ref[k * TS:(k + 1) * TS, b * W:(b + 1) * W] = out[b * TS:(b + 1) * TS]

    ar, ai = ab_ref[0], ab_ref[1]
    project(0)
    if nsub > 1:
        project(1)
    y_pending = None
    for k in range(nsub):
        zk = z_ref.at[k]
        zr, zi = st_ref[0], st_ref[1]
        for t in range(TS):
            rs = slice(t * SUBLANES, (t + 1) * SUBLANES)
            zr, zi = (ar * zr - ai * zi + zk[rs, :N], ar * zi + ai * zr + zk[rs, N:])
            zk[rs, :N] = zr
            zk[rs, N:] = zi
        st_ref[0] = zr
        st_ref[1] = zi

        half = SUBLANES * TS // 2
        y_tb = jnp.concatenate([_dot(zk[:half, :].astype(BF16), cm_ref[...]),
                                _dot(zk[half:, :].astype(BF16), cm_ref[...])], axis=0)
        if k + 2 < nsub:
            project(k + 2)
        if y_pending is not None:
            glu_store(k - 1, y_pending)
        y_hi, y_lo = _split2(y_tb)
        y = _dot(permt_ref[...], y_hi) + _dot(permt_ref[...], y_lo)
        y_pending = _gelu_tanh(y + d_ref[...] * slab(k))
    glu_store(nsub - 1, y_pending)


def _s5(u, perm, permt, bb, cm, ab, d, wglu, bglu):
    S, BW = u.shape
    W = GROUP_WIDTH
    assert BW == SUBLANES * W, "the S5 kernel maps the batch onto the 8 sublanes"
    rows = SUBLANES * TS_S5
    Lt = TS_S5 * NSUB_S5
    N = S5_LANES
    return pl.pallas_call(
        _s5_kernel,
        out_shape=jax.ShapeDtypeStruct((S, BW), BF16),
        grid=(S // Lt,),
        in_specs=[
            pl.BlockSpec((Lt, BW), lambda t: (t, 0)),
            _const_spec((rows, rows)),
            _const_spec((rows, rows)),
            _const_spec((W, 2 * N)),
            _const_spec((2 * N, W)),
            _const_spec((2, SUBLANES, N)),
            _const_spec((1, W)),
            _const_spec((W, W)),
            _const_spec((1, W)),
        ],
        out_specs=pl.BlockSpec((Lt, BW), lambda t: (t, 0)),
        scratch_shapes=[pltpu.VMEM((NSUB_S5, rows, 2 * N), F32), pltpu.VMEM((2, SUBLANES, N), F32)],
        compiler_params=_params("arbitrary"),
        name="s5",
    )(u, perm, permt, bb, cm, ab, d, wglu, bglu)


def _outffn_kernel(x_ref, oa_ref, ob_ref, oc_ref, od_ref, wo_ref, g1_ref, g2_ref, g3_ref,
                   w1_ref, w2_ref, out_ref):
    mixin = jnp.concatenate([oa_ref[...], ob_ref[...], oc_ref[...], od_ref[...]], axis=-1)
    x1 = x_ref[...] + _rms(_dot(mixin, wo_ref[...]), g1_ref[...])
    h = _rms(x1, g2_ref[...]).astype(BF16)
    f = jnp.zeros_like(x1)
    for c in range(D_FF // FF_CHUNK):
        a = jnp.maximum(_dot(h, w1_ref[:, c * FF_CHUNK:(c + 1) * FF_CHUNK]), 0.0)
        f = f + _dot((a * a).astype(BF16), w2_ref[c * FF_CHUNK:(c + 1) * FF_CHUNK, :])
    out_ref[...] = x1 + _rms(f, g3_ref[...])


def _outffn(x, oa, ob, oc, od, wo, g1, g2, g3, w1, w2):
    B, S, D = x.shape
    tm = TM_FFN
    W = GROUP_WIDTH
    row = lambda n: pl.BlockSpec((None, tm, n), lambda b, t: (b, t, 0))
    once = lambda shape: pl.BlockSpec(shape, lambda b, t: (0, 0), pipeline_mode=pl.Buffered(1))
    return pl.pallas_call(
        _outffn_kernel,
        out_shape=jax.ShapeDtypeStruct((B, S, D), F32),
        grid=(B, S // tm),
        in_specs=[row(D), row(W), row(W), pl.BlockSpec((tm, W), lambda b, t: (t, b)), row(W),
                  once((D, D)), _const_spec((1, D)), _const_spec((1, D)), _const_spec((1, D)),
                  once((D, D_FF)), once((D_FF, D))],
        out_specs=row(D),
        compiler_params=_params("parallel", "parallel"),
        name="outffn",
    )(x, oa, ob, oc, od, wo, g1, g2, g3, w1, w2)


def _rope_tables(S):
    inv = ROPE_THETA ** (-jnp.arange(0, DIFF_DH, 2, dtype=F32) / DIFF_DH)
    ang = jnp.arange(S, dtype=F32)[:, None] * inv[None, :]
    cos = jnp.concatenate([jnp.cos(ang), jnp.cos(ang)], axis=-1)
    sin = jnp.concatenate([-jnp.sin(ang), jnp.sin(ang)], axis=-1)
    reps = GROUP_WIDTH // DIFF_DH
    return jnp.tile(cos, (1, reps)), jnp.tile(sin, (1, reps))


def _block_diag(blocks):
    n, a, b = blocks.shape
    eye = jnp.eye(n, dtype=blocks.dtype)
    return (eye[:, None, :, None] * blocks[:, :, None, :]).reshape(n * a, n * b)


def _s5_perm():
    n = SUBLANES * TS_S5
    r = np.arange(n)
    src = (r % SUBLANES) * TS_S5 + r // SUBLANES
    p = np.zeros((n, n), np.float32)
    p[r, src] = 1.0
    return p


def kernel(x, ln_mix_pre, ln_mix_post, ln_ffn_pre, ln_ffn_post, w_in, w_out, gla_w_gate, gla_b_gate, gla_norm, diff_lq1, diff_lk1, diff_lq2, diff_lk2, diff_norm, s5_log_step, s5_a_re, s5_a_im, s5_b_re, s5_b_im, s5_c_re, s5_c_im, s5_d, s5_w_glu, s5_b_glu, lru_conv_w, lru_conv_b, lru_w_a, lru_b_a, lru_w_x, lru_b_x, lru_lambda, ffn_w1, ffn_w2):
    B, S, D = x.shape
    L = w_in.shape[0]
    W = GROUP_WIDTH
    HK = GLA_HEADS * GLA_DK

    cos_n, sin_n = _rope_tables(S)
    q_scale = DIFF_DH ** -0.5 * math.log2(math.e)
    cos_t = (cos_n * q_scale).T
    sin_t = (sin_n * q_scale).T
    tril = jnp.asarray(np.kron(np.eye(LT_GLA // GLA_CHUNK), np.tril(np.ones((GLA_CHUNK, GLA_CHUNK)))), BF16)
    gmean = jnp.asarray(np.kron(np.eye(GLA_HEADS), np.full((GLA_DV, GLA_DV), 1.0 / GLA_DV)), BF16)
    perm = _s5_perm()
    perm_b, permt_b = jnp.asarray(perm, BF16), jnp.asarray(perm.T, BF16)

    ls_e = jnp.repeat(s5_log_step, S5_STATE, axis=1)[:, None, :]
    lr_e = s5_a_re.reshape(L, 1, S5_LANES)
    li_e = s5_a_im.reshape(L, 1, S5_LANES)
    br_e = s5_b_re.transpose(0, 3, 1, 2).reshape(L, S5_CH, S5_LANES)
    bi_e = s5_b_im.transpose(0, 3, 1, 2).reshape(L, S5_CH, S5_LANES)
    bbr, bbi, ab = _s5_prep(ls_e, lr_e, li_e, br_e, bi_e)
    grp = jnp.arange(S5_LANES) // S5_STATE
    gsel = (jnp.arange(S5_GROUPS)[:, None] == grp[None, :]).astype(F32)

    for l in range(L):
        lam_init = 0.8 - 0.6 * math.exp(-0.3 * l)
        wi = w_in[l]
        offs = np.cumsum([0, HK, HK, 2 * HK, GLA_RANK, 2 * HK, W, W, W, W, W, W])
        seg = [wi[:, offs[i]:offs[i + 1]] for i in range(11)]
        g_q, g_k, g_v, g_lr, g_og, d_q, d_k, d_v, s_u, r_x, r_g = seg
        glr_pad = jnp.pad(g_lr, ((0, 0), (0, LANES - GLA_RANK)))
        wn = jnp.concatenate([g_q, g_k, g_v, g_og, glr_pad, d_k, s_u, r_x, r_g], axis=1).astype(BF16)
        wt = jnp.concatenate([d_q, d_v], axis=1).T.astype(BF16)

        gla_p, kd, qt, vt, su, o_d = _inproj(
            x, ln_mix_pre[l][None], wn, wt, cos_n, sin_n, cos_t, sin_t,
            lru_conv_w[l], lru_conv_b[l][None],
            _block_diag(lru_w_a[l]).astype(BF16), lru_b_a[l][None],
            _block_diag(lru_w_x[l]).astype(BF16), lru_b_x[l][None], lru_lambda[l][None])

        wg = jnp.pad(gla_w_gate[l], ((0, LANES - GLA_RANK), (0, 0))).astype(BF16)
        o_a = _gla(gla_p, wg, gla_b_gate[l][None], tril, gmean,
                   jnp.tile(gla_norm[l], GLA_HEADS)[None])

        lq = jnp.stack([diff_lq1[l], diff_lq2[l]])
        lk = jnp.stack([diff_lk1[l], diff_lk2[l]])
        gain_t = jnp.broadcast_to(jnp.tile(diff_norm[l], DIFF_HEADS)[:, None], (W, T_ATT))
        o_b = _diff(qt, kd, vt, lq, lk, gain_t, lam_init)

        bb = jnp.concatenate(
            [(gsel[:, None, :] * b[None, :, :]).reshape(W, S5_LANES) for b in (bbr[l], bbi[l])],
            axis=1).astype(BF16)
        cm = jnp.concatenate([_block_diag(s5_c_re[l]).T, -_block_diag(s5_c_im[l]).T], axis=0).astype(BF16)
        o_c = _s5(su, perm_b, permt_b, bb, cm, ab[l], s5_d[l][None],
                  s5_w_glu[l].astype(BF16), s5_b_glu[l][None])

        x = _outffn(x, o_a, o_b, o_c, o_d, w_out[l].astype(BF16),
                    ln_mix_post[l][None], ln_ffn_pre[l][None], ln_ffn_post[l][None],
                    ffn_w1[l].astype(BF16), ffn_w2[l].astype(BF16))
    return x
```

```python
import functools
import math

import numpy as np
import jax
import jax.numpy as jnp
from jax import lax
from jax.experimental import pallas as pl
from jax.experimental.pallas import tpu as pltpu

F32 = jnp.float32
BF16 = jnp.bfloat16

D_MODEL = 1024
GROUP_WIDTH = 256
EPS = 1e-6
GLA_HEADS, GLA_DK, GLA_DV, GLA_RANK, GLA_CHUNK = 4, 32, 64, 16, 64
GLA_GATE_NORM = 16.0
DIFF_HEADS, DIFF_DH, DIFF_DV = 4, 32, 64
ROPE_THETA = 10000.0
S5_GROUPS, S5_STATE, S5_CH = 16, 64, 16
S5_LANES = S5_GROUPS * S5_STATE
LRU_BLOCKS, LRU_BLOCK, LRU_CONV, LRU_C = 4, 64, 4, 8.0
D_FF = 4 * D_MODEL

SUBLANES = 8
LANES = 128
VMEM_LIMIT = 56 * 1024 * 1024

TM_INPROJ = 512
LRU_SCAN_ROWS = 64
T_ATT = 512
LT_GLA = 256
NB_GLA = 4
TS_S5 = 32
NSUB_S5 = 8
TM_FFN = 512
FFN_SUBTILES = 2
FF_CHUNK = 1024


def _dot(a, b):
    return jnp.dot(a, b, preferred_element_type=F32)


def _dot_nt(a, b):
    return lax.dot_general(a, b, (((1,), (1,)), ((), ())), preferred_element_type=F32)


def _dot_tn(a, b):
    return lax.dot_general(a, b, (((0,), (0,)), ((), ())), preferred_element_type=F32)


def _split2(x):
    hi = x.astype(BF16)
    lo = (x - hi.astype(F32)).astype(BF16)
    return hi, lo


def _split3(x):
    hi = x.astype(BF16)
    r = x - hi.astype(F32)
    mid = r.astype(BF16)
    lo = (r - mid.astype(F32)).astype(BF16)
    return hi, mid, lo


def _sigmoid(x):
    return 1.0 / (1.0 + jnp.exp(-x))


def _softplus(x):
    return jnp.maximum(x, 0.0) + jnp.log1p(jnp.exp(-jnp.abs(x)))


def _gelu_tanh(x):
    c = math.sqrt(2.0 / math.pi)
    return 0.5 * x * (1.0 + jnp.tanh(c * (x + 0.044715 * (x * x * x))))


def _rms(x, g):
    ms = jnp.mean(x * x, axis=-1, keepdims=True)
    return x * lax.rsqrt(ms + EPS) * g


def _params(*sem):
    return pltpu.CompilerParams(dimension_semantics=sem, vmem_limit_bytes=VMEM_LIMIT)


def _const_spec(shape):
    nd = len(shape)
    return pl.BlockSpec(shape, lambda *_: (0,) * nd)


N_GLA = 896
N_NAT = N_GLA + 4 * GROUP_WIDTH
N_T = 2 * GROUP_WIDTH


def _shift_rows(x, d, fill):
    n = x.shape[0]
    if d % SUBLANES == 0:
        return jnp.concatenate([jnp.full((d, x.shape[1]), fill, x.dtype), x[:n - d]], axis=0)
    r = lax.broadcasted_iota(jnp.int32, x.shape, 0)
    return jnp.where(r >= d, pltpu.roll(x, d, 0), fill)


def _inproj_kernel(x_ref, g_ref, wn_ref, wt_ref, cn_ref, sn_ref, ct_ref, st_ref,
                   cw_ref, cb_ref, wa_ref, ba_ref, wx_ref, bx_ref, lam_ref,
                   gla_ref, kd_ref, qt_ref, vt_ref, su_ref, od_ref, tail_ref, hst_ref):
    tm = x_ref.shape[0]
    W = GROUP_WIDTH
    HALF = DIFF_DH // 2

    @pl.when(pl.program_id(1) == 0)
    def _():
        tail_ref[...] = jnp.zeros_like(tail_ref)
        hst_ref[...] = jnp.zeros_like(hst_ref)

    h = _rms(x_ref[...], g_ref[...]).astype(BF16)
    o = N_GLA
    rxg = _dot(h, wn_ref[:, o + 2 * W:o + 4 * W])
    xb, gate = rxg[:, :W], rxg[:, W:]
    ext = jnp.concatenate([tail_ref[...], xb], axis=0)
    tail_ref[...] = xb[tm - SUBLANES:]
    xc = xb * cw_ref[LRU_CONV - 1:LRU_CONV, :] + cb_ref[...]
    for sh in range(1, LRU_CONV):
        xs = pltpu.roll(ext, sh, 0)[SUBLANES:]
        xc = xc + xs * cw_ref[LRU_CONV - 1 - sh:LRU_CONV - sh, :]
    xcb = xc.astype(BF16)

    k = _dot(h, wn_ref[:, o:o + W])
    first_half = lax.broadcasted_iota(jnp.int32, k.shape, 1) % DIFF_DH < HALF
    k_sw = jnp.where(first_half, pltpu.roll(k, W - HALF, 1), pltpu.roll(k, HALF, 1))
    kd_ref[...] = (k * cn_ref[...] + k_sw * sn_ref[...]).astype(BF16)

    r = _sigmoid(_dot(xcb, wa_ref[...]) + ba_ref[...])
    ig = _sigmoid(_dot(xcb, wx_ref[...]) + bx_ref[...])

    gla_ref[...] = _dot(h, wn_ref[:, :N_GLA])
    su_ref[...] = _dot(h, wn_ref[:, o + W:o + 2 * W])
    q = _dot_nt(wt_ref[:W, :], h)
    q_sw = jnp.concatenate(
        [q[g * DIFF_DH + off:g * DIFF_DH + off + HALF] for g in range(W // DIFF_DH) for off in (HALF, 0)],
        axis=0)
    qt_ref[...] = (q * ct_ref[...] + q_sw * st_ref[...]).astype(BF16)
    vt = _dot_nt(wt_ref[W:, :], h).astype(BF16)
    for j in range(vt_ref.shape[0]):
        vt_ref[j] = vt[:, j * T_ATT:(j + 1) * T_ATT]

    sp = _softplus(-lam_ref[...])
    h_prev = hst_ref[...]
    for s0 in range(0, tm, LRU_SCAN_ROWS):
        rows = slice(s0, s0 + LRU_SCAN_ROWS)
        log_a = -LRU_C * r[rows] * sp
        a = jnp.exp(log_a)
        th = jnp.tanh(log_a)
        mult = jnp.sqrt(jnp.maximum(-2.0 * th / (1.0 - th), 1e-12))
        b = mult * (ig[rows] * xc[rows])
        d = 1
        while d < LRU_SCAN_ROWS:
            b = b + a * _shift_rows(b, d, 0.0)
            a = a * _shift_rows(a, d, 1.0)
            d *= 2
        hs = b + a * h_prev
        h_prev = hs[LRU_SCAN_ROWS - 1:LRU_SCAN_ROWS]
        od_ref[rows, :] = (hs * _gelu_tanh(gate[rows])).astype(BF16)
    hst_ref[...] = h_prev


def _inproj(x, g, wn, wt, cn, sn, ct, st, cw, cb, wa, ba, wx, bx, lam):
    B, S, D = x.shape
    tm = TM_INPROJ
    W = GROUP_WIDTH
    nsub = tm // T_ATT
    out_shape = (
        jax.ShapeDtypeStruct((B, S, N_GLA), F32),
        jax.ShapeDtypeStruct((B, S, W), BF16),
        jax.ShapeDtypeStruct((B, W, S), BF16),
        jax.ShapeDtypeStruct((B, S // T_ATT, W, T_ATT), BF16),
        jax.ShapeDtypeStruct((S, B * W), F32),
        jax.ShapeDtypeStruct((B, S, W), BF16),
    )
    row = lambda n: pl.BlockSpec((None, tm, n), lambda b, t: (b, t, 0))
    return pl.pallas_call(
        _inproj_kernel,
        out_shape=out_shape,
        grid=(B, S // tm),
        in_specs=[
            row(D),
            _const_spec((1, D)),
            _const_spec((D, N_NAT)),
            _const_spec((N_T, D)),
            pl.BlockSpec((tm, W), lambda b, t: (t, 0)),
            pl.BlockSpec((tm, W), lambda b, t: (t, 0)),
            pl.BlockSpec((W, tm), lambda b, t: (0, t)),
            pl.BlockSpec((W, tm), lambda b, t: (0, t)),
            _const_spec((LRU_CONV, W)),
            _const_spec((1, W)),
            _const_spec((W, W)),
            _const_spec((1, W)),
            _const_spec((W, W)),
            _const_spec((1, W)),
            _const_spec((1, W)),
        ],
        out_specs=(
            row(N_GLA),
            row(W),
            pl.BlockSpec((None, W, tm), lambda b, t: (b, 0, t)),
            pl.BlockSpec((None, nsub, W, T_ATT), lambda b, t: (b, t, 0, 0)),
            pl.BlockSpec((tm, W), lambda b, t: (t, b)),
            row(W),
        ),
        scratch_shapes=[pltpu.VMEM((SUBLANES, W), F32), pltpu.VMEM((1, W), F32)],
        compiler_params=_params("parallel", "arbitrary"),
        name="inproj",
    )(x, g, wn, wt, cn, sn, ct, st, cw, cb, wa, ba, wx, bx, lam)


def _gla_kernel(p_ref, wg_ref, bg_ref, tril_ref, gm_ref, gain_ref, o_ref, st_ref):
    NB, Lt, _ = p_ref.shape
    C = GLA_CHUNK
    HK = GLA_HEADS * GLA_DK
    HV = GLA_HEADS * GLA_DV

    @pl.when(pl.program_id(1) == 0)
    def _():
        st_ref[...] = jnp.zeros_like(st_ref)

    lane_k = lax.broadcasted_iota(jnp.int32, (C, HK), 1) // GLA_DK
    r_a = lax.broadcasted_iota(jnp.int32, (GLA_HEADS * C, C), 0) % C
    c_a = lax.broadcasted_iota(jnp.int32, (GLA_HEADS * C, C), 1)
    causal = r_a >= c_a
    lane_v = lax.broadcasted_iota(jnp.int32, (C, HV), 1) // GLA_DV
    st_row = lax.broadcasted_iota(jnp.int32, (HV, HK), 0) // GLA_DV
    st_col = lax.broadcasted_iota(jnp.int32, (HV, HK), 1) // GLA_DK
    same_head = st_row == st_col
    tril = tril_ref[...]

    q_dec, k_dec, k_raw, v, bcum = [], [], [], [], []
    for b in range(NB):
        k = p_ref[b, :, HK:2 * HK]
        glr = p_ref[b, :, 2 * HK + 2 * HV:].astype(BF16)
        g = -_softplus(-(_dot(glr, wg_ref[...]) + bg_ref[...])) * (1.0 / GLA_GATE_NORM)
        g_hi, g_lo = _split2(g)
        bc = _dot(tril, g_hi) + _dot(tril, g_lo)
        q_dec.append((p_ref[b, :, 0:HK] * (GLA_DK ** -0.5) * jnp.exp(bc)).astype(BF16))
        k_dec.append((k * jnp.exp(-bc)).astype(BF16))
        k_raw.append(k)
        v.append(p_ref[b, :, 2 * HK:2 * HK + HV].astype(BF16))
        bcum.append(bc)

    outs = [[] for _ in range(NB)]
    rng = range(NB)
    for c in range(Lt // C):
        sl = slice(c * C, (c + 1) * C)
        blast = [bcum[b][sl][C - 1:C, :] for b in rng]
        scores = []
        for b in rng:
            qd = q_dec[b][sl]
            q_big = jnp.concatenate(
                [jnp.where(lane_k == h, qd, jnp.zeros_like(qd)) for h in range(GLA_HEADS)], axis=0)
            scores.append(_dot_nt(q_big, k_dec[b][sl]))
        state = [st_ref[b] for b in rng]
        o_inter = [_dot_nt(q_dec[b][sl], state[b].astype(BF16)) for b in rng]
        kv_t = []
        for b in rng:
            k_st = (k_raw[b][sl] * jnp.exp(blast[b] - bcum[b][sl])).astype(BF16)
            kv_t.append(_dot_tn(v[b][sl], k_st))
        o_all = [_dot(jnp.where(causal, scores[b], 0.0).astype(BF16), v[b][sl]) for b in rng]
        for b in rng:
            o_c = o_inter[b]
            for h in range(GLA_HEADS):
                o_c = o_c + jnp.where(lane_v == h, o_all[b][h * C:(h + 1) * C], 0.0)
            st_ref[b] = state[b] * jnp.exp(blast[b]) + jnp.where(same_head, kv_t[b], 0.0)
            outs[b].append(o_c)
    for b in range(NB):
        o = jnp.concatenate(outs[b], axis=0)
        sq_hi, sq_lo = _split2(o * o)
        ms = _dot(sq_hi, gm_ref[...]) + _dot(sq_lo, gm_ref[...])
        o = o * lax.rsqrt(ms + EPS) * gain_ref[...]
        og = p_ref[b, :, 2 * HK + HV:2 * HK + 2 * HV]
        o_ref[b] = (o * (og * _sigmoid(og))).astype(BF16)


def _gla(p, wg, bg, tril, gm, gain):
    B, S, _ = p.shape
    Lt = LT_GLA
    NB = NB_GLA
    HV = GLA_HEADS * GLA_DV
    HK = GLA_HEADS * GLA_DK
    return pl.pallas_call(
        _gla_kernel,
        out_shape=jax.ShapeDtypeStruct((B, S, HV), BF16),
        grid=(B // NB, S // Lt),
        in_specs=[
            pl.BlockSpec((NB, Lt, N_GLA), lambda b, t: (b, t, 0)),
            _const_spec((HK, HK)),
            _const_spec((1, HK)),
            _const_spec((Lt, Lt)),
            _const_spec((HV, HV)),
            _const_spec((1, HV)),
        ],
        out_specs=pl.BlockSpec((NB, Lt, HV), lambda b, t: (b, t, 0)),
        scratch_shapes=[pltpu.VMEM((NB, HV, HK), F32)],
        compiler_params=_params("parallel", "arbitrary"),
        name="gla",
    )(p, wg, bg, tril, gm, gain)


V_AUG = DIFF_DV + 16

def _diff_kernel(qt_ref, k_ref, vt_ref, lq_ref, lk_ref, gain_ref, o_ref, wq_ref, acc_ref, m_ref, s_ref,
                 mx_ref, *, lam_init):
    T = T_ATT
    NB = 2 * DIFF_HEADS
    i = pl.program_id(1)
    @pl.when((pl.program_id(0) == 0) & (i == 0))
    def _():
        wq_ref[...] = jnp.zeros_like(wq_ref)

    for n in range(NB):
        rows = slice(n * DIFF_DH, (n + 1) * DIFF_DH)
        wq_ref[n // 2, rows, (n % 2) * T:(n % 2 + 1) * T] = qt_ref[rows, :]
    ones = jnp.ones((V_AUG - DIFF_DV, T), BF16)

    def produce(j, h, buf, diagonal):
        kb = k_ref[pl.ds(pl.multiple_of(j * T, T), T), :]
        s = _dot(kb, wq_ref[h])
        if diagonal:
            kk = lax.broadcasted_iota(jnp.int32, s.shape, 0)
            qq = lax.broadcasted_iota(jnp.int32, s.shape, 1)
            s = jnp.where(kk <= jnp.where(qq >= T, qq - T, qq), s, -1e30)
        for c in range(2):
            sc = s[:, c * T:(c + 1) * T]
            s_ref[buf, 2 * h + c] = sc
            mx_ref[buf, 2 * h + c] = jnp.max(sc, axis=0, keepdims=True)

    def block(j, j_next, cons, prod, first=False):
        for h in range(DIFF_HEADS):
            if first:
                produce(j, h, cons, True)
            produce(j_next, h, prod, False)
            va = jnp.concatenate([vt_ref[j, h * DIFF_DV:(h + 1) * DIFF_DV, :], ones], axis=0)
            for c in range(2):
                n = 2 * h + c
                if first:
                    m_new = mx_ref[cons, n]
                    acc_ref[n] = _dot(va, jnp.exp2(s_ref[cons, n] - m_new).astype(BF16))
                else:
                    m = m_ref[n]
                    m_new = jnp.maximum(m, mx_ref[cons, n])
                    p = jnp.exp2(s_ref[cons, n] - m_new).astype(BF16)
                    acc_ref[n] = jnp.exp2(m - m_new) * acc_ref[n] + _dot(va, p)
                m_ref[n] = m_new

    block(i, 0, 0, 1, first=True)

    def body(t, carry):
        block(2 * t, 2 * t + 1, 1, 0)
        block(2 * t + 1, jnp.minimum(2 * t + 2, i - 1), 0, 1)
        return carry

    lax.fori_loop(0, i >> 1, body, 0)

    @pl.when((i & 1) == 1)
    def _():
        block(i - 1, i - 1, 1, 0)

    e = jnp.exp(jnp.sum(lq_ref[...] * lk_ref[...], axis=-1, keepdims=True))
    lam = e[0:1] - e[1:2] + lam_init
    heads = []
    for h in range(DIFF_HEADS):
        a1, a2 = acc_ref[2 * h], acc_ref[2 * h + 1]
        o1 = a1[:DIFF_DV] / a1[DIFF_DV:DIFF_DV + 1]
        o2 = a2[:DIFF_DV] / a2[DIFF_DV:DIFF_DV + 1]
        oh = o1 - lam * o2
        ms = jnp.mean(oh * oh, axis=0, keepdims=True)
        heads.append(oh * lax.rsqrt(ms + EPS))
    ot = jnp.concatenate(heads, axis=0) * gain_ref[...] * (1.0 - lam_init)
    o_ref[...] = ot.T.astype(BF16)


def _diff(qt, k, vt, lq, lk, gain_t, lam_init):
    B, W, S = qt.shape
    T = T_ATT
    return pl.pallas_call(
        functools.partial(_diff_kernel, lam_init=lam_init),
        out_shape=jax.ShapeDtypeStruct((B, S, W), BF16),
        grid=(B, S // T),
        in_specs=[
            pl.BlockSpec((None, W, T), lambda b, i: (b, 0, i)),
            pl.BlockSpec((None, S, W), lambda b, i: (b, 0, 0)),
            pl.BlockSpec((None, S // T, W, T), lambda b, i: (b, 0, 0, 0)),
            _const_spec((2, DIFF_DH)),
            _const_spec((2, DIFF_DH)),
            _const_spec((W, T)),
        ],
        out_specs=pl.BlockSpec((None, T, W), lambda b, i: (b, i, 0)),
        scratch_shapes=[pltpu.VMEM((DIFF_HEADS, W, 2 * T), BF16),
                        pltpu.VMEM((2 * DIFF_HEADS, V_AUG, T), F32),
                        pltpu.VMEM((2 * DIFF_HEADS, 1, T), F32),
                        pltpu.VMEM((2, 2 * DIFF_HEADS, T, T), F32),
                        pltpu.VMEM((2, 2 * DIFF_HEADS, 1, T), F32)],
        compiler_params=_params("arbitrary", "arbitrary"),
        name="diff",
    )(qt, k, vt, lq, lk, gain_t)


def _s5_prep_kernel(ls_ref, lr_ref, li_ref, br_ref, bi_ref,
                    bbr_ref, bbi_ref, ab_ref):
    step = jnp.exp(ls_ref[...])
    lr, li = lr_ref[...], li_ref[...]
    mag = jnp.exp(lr * step)
    ab_re = mag * jnp.cos(li * step)
    ab_im = mag * jnp.sin(li * step)
    den = lr * lr + li * li
    nr, ni = ab_re - 1.0, ab_im
    cr = (nr * lr + ni * li) / den
    ci = (ni * lr - nr * li) / den
    bbr_ref[...] = cr * br_ref[...] - ci * bi_ref[...]
    bbi_ref[...] = cr * bi_ref[...] + ci * br_ref[...]

    ab_ref[0] = jnp.broadcast_to(ab_re, (SUBLANES, S5_LANES))
    ab_ref[1] = jnp.broadcast_to(ab_im, (SUBLANES, S5_LANES))


def _s5_prep(ls_e, lr, li, br_e, bi_e):
    L = ls_e.shape[0]
    N = S5_LANES
    vec = pl.BlockSpec((None, 1, N), lambda l: (l, 0, 0))
    mat = pl.BlockSpec((None, S5_CH, N), lambda l: (l, 0, 0))
    return pl.pallas_call(
        _s5_prep_kernel,
        out_shape=(
            jax.ShapeDtypeStruct((L, S5_CH, N), F32),
            jax.ShapeDtypeStruct((L, S5_CH, N), F32),
            jax.ShapeDtypeStruct((L, 2, SUBLANES, N), F32),
        ),
        grid=(L,),
        in_specs=[vec, vec, vec, mat, mat],
        out_specs=(mat, mat, pl.BlockSpec((None, 2, SUBLANES, N), lambda l: (l, 0, 0, 0))),
        compiler_params=_params("arbitrary"),
        name="s5_prep",
    )(ls_e, lr, li, br_e, bi_e)


def _s5_kernel(u_ref, perm_ref, permt_ref, bb_ref, cm_ref, ab_ref, d_ref, wglu_ref, bglu_ref,
               o_ref, z_ref, st_ref):
    N = S5_LANES
    W = GROUP_WIDTH
    TS = TS_S5
    nsub = u_ref.shape[0] // TS

    @pl.when(pl.program_id(0) == 0)
    def _():
        st_ref[...] = jnp.zeros_like(st_ref)

    def slab(k):
        return jnp.concatenate(
            [u_ref[k * TS:(k + 1) * TS, b * W:(b + 1) * W] for b in range(SUBLANES)], axis=0)

    def project(k):
        u_tb = _dot(perm_ref[...], slab(k).astype(BF16)).astype(BF16)
        z_ref[k] = _dot(u_tb, bb_ref[...])

    def glu_store(k, y):
        gate = _sigmoid(_dot(y.astype(BF16), wglu_ref[...]) + bglu_ref[...])
        out = (y * gate).astype(BF16)
        for b in range(SUBLANES):
            o_ref[k * TS:(k + 1) * TS, b * W:(b + 1) * W] = out[b * TS:(b + 1) * TS]

    ar, ai = ab_ref[0], ab_ref[1]
    project(0)
    if nsub > 1:
        project(1)
    y_pending = None
    for k in range(nsub):
        zk = z_ref.at[k]
        zr, zi = st_ref[0], st_ref[1]
        for t in range(TS):
            rs = slice(t * SUBLANES, (t + 1) * SUBLANES)
            zr, zi = (ar * zr - ai * zi + zk[rs, :N], ar * zi + ai * zr + zk[rs, N:])
            zk[rs, :N] = zr
            zk[rs, N:] = zi
        st_ref[0] = zr
        st_ref[1] = zi

        half = SUBLANES * TS // 2
        y_tb = jnp.concatenate([_dot(zk[:half, :].astype(BF16), cm_ref[...]),
                                _dot(zk[half:, :].astype(BF16), cm_ref[...])], axis=0)
        if k + 2 < nsub:
            project(k + 2)
        if y_pending is not None:
            glu_store(k - 1, y_pending)
        y_hi, y_lo = _split2(y_tb)
        y = _dot(permt_ref[...], y_hi) + _dot(permt_ref[...], y_lo)
        y_pending = _gelu_tanh(y + d_ref[...] * slab(k))
    glu_store(nsub - 1, y_pending)


def _s5(u, perm, permt, bb, cm, ab, d, wglu, bglu):
    S, BW = u.shape
    W = GROUP_WIDTH
    assert BW == SUBLANES * W, "the S5 kernel maps the batch onto the 8 sublanes"
    rows = SUBLANES * TS_S5
    Lt = TS_S5 * NSUB_S5
    N = S5_LANES
    return pl.pallas_call(
        _s5_kernel,
        out_shape=jax.ShapeDtypeStruct((S, BW), BF16),
        grid=(S // Lt,),
        in_specs=[
            pl.BlockSpec((Lt, BW), lambda t: (t, 0)),
            _const_spec((rows, rows)),
            _const_spec((rows, rows)),
            _const_spec((W, 2 * N)),
            _const_spec((2 * N, W)),
            _const_spec((2, SUBLANES, N)),
            _const_spec((1, W)),
            _const_spec((W, W)),
            _const_spec((1, W)),
        ],
        out_specs=pl.BlockSpec((Lt, BW), lambda t: (t, 0)),
        scratch_shapes=[pltpu.VMEM((NSUB_S5, rows, 2 * N), F32), pltpu.VMEM((2, SUBLANES, N), F32)],
        compiler_params=_params("arbitrary"),
        name="s5",
    )(u, perm, permt, bb, cm, ab, d, wglu, bglu)


def _outffn_kernel(x_ref, oa_ref, ob_ref, oc_ref, od_ref, wo_ref, g1_ref, g2_ref, g3_ref,
                   w1_ref, w2_ref, out_ref):
    tm = x_ref.shape[0]
    subs = [slice(s0, s0 + tm // FFN_SUBTILES) for s0 in range(0, tm, tm // FFN_SUBTILES)]
    mix = [_dot(jnp.concatenate([oa_ref[r, :], ob_ref[r, :], oc_ref[r, :], od_ref[r, :]], axis=-1),
                wo_ref[...]) for r in subs]
    x1 = [x_ref[r, :] + _rms(m, g1_ref[...]) for r, m in zip(subs, mix)]
    h = [_rms(v, g2_ref[...]).astype(BF16) for v in x1]
    f = [jnp.zeros_like(v) for v in x1]
    for c in range(D_FF // FF_CHUNK):
        cols = slice(c * FF_CHUNK, (c + 1) * FF_CHUNK)
        a = [jnp.maximum(_dot(hv, w1_ref[:, cols]), 0.0) for hv in h]
        f = [fv + _dot((av * av).astype(BF16), w2_ref[cols, :]) for fv, av in zip(f, a)]
    for r, v, fv in zip(subs, x1, f):
        out_ref[r, :] = v + _rms(fv, g3_ref[...])


def _outffn(x, oa, ob, oc, od, wo, g1, g2, g3, w1, w2):
    B, S, D = x.shape
    tm = TM_FFN
    W = GROUP_WIDTH
    row = lambda n: pl.BlockSpec((None, tm, n), lambda b, t: (b, t, 0))
    once = lambda shape: pl.BlockSpec(shape, lambda b, t: (0, 0), pipeline_mode=pl.Buffered(1))
    return pl.pallas_call(
        _outffn_kernel,
        out_shape=jax.ShapeDtypeStruct((B, S, D), F32),
        grid=(B, S // tm),
        in_specs=[row(D), row(W), row(W), pl.BlockSpec((tm, W), lambda b, t: (t, b)), row(W),
                  once((D, D)), _const_spec((1, D)), _const_spec((1, D)), _const_spec((1, D)),
                  once((D, D_FF)), once((D_FF, D))],
        out_specs=row(D),
        compiler_params=_params("parallel", "parallel"),
        name="outffn",
    )(x, oa, ob, oc, od, wo, g1, g2, g3, w1, w2)


def _rope_tables(S):
    inv = ROPE_THETA ** (-jnp.arange(0, DIFF_DH, 2, dtype=F32) / DIFF_DH)
    ang = jnp.arange(S, dtype=F32)[:, None] * inv[None, :]
    cos = jnp.concatenate([jnp.cos(ang), jnp.cos(ang)], axis=-1)
    sin = jnp.concatenate([-jnp.sin(ang), jnp.sin(ang)], axis=-1)
    reps = GROUP_WIDTH // DIFF_DH
    return jnp.tile(cos, (1, reps)), jnp.tile(sin, (1, reps))


def _block_diag(blocks):
    n, a, b = blocks.shape
    eye = jnp.eye(n, dtype=blocks.dtype)
    return (eye[:, None, :, None] * blocks[:, :, None, :]).reshape(n * a, n * b)


def _s5_perm():
    n = SUBLANES * TS_S5
    r = np.arange(n)
    src = (r % SUBLANES) * TS_S5 + r // SUBLANES
    p = np.zeros((n, n), np.float32)
    p[r, src] = 1.0
    return p


def kernel(x, ln_mix_pre, ln_mix_post, ln_ffn_pre, ln_ffn_post, w_in, w_out, gla_w_gate, gla_b_gate, gla_norm, diff_lq1, diff_lk1, diff_lq2, diff_lk2, diff_norm, s5_log_step, s5_a_re, s5_a_im, s5_b_re, s5_b_im, s5_c_re, s5_c_im, s5_d, s5_w_glu, s5_b_glu, lru_conv_w, lru_conv_b, lru_w_a, lru_b_a, lru_w_x, lru_b_x, lru_lambda, ffn_w1, ffn_w2):
    B, S, D = x.shape
    L = w_in.shape[0]
    W = GROUP_WIDTH
    HK = GLA_HEADS * GLA_DK

    cos_n, sin_n = _rope_tables(S)
    q_scale = DIFF_DH ** -0.5 * math.log2(math.e)
    cos_t = (cos_n * q_scale).T
    sin_t = (sin_n * q_scale).T
    tril = jnp.asarray(np.kron(np.eye(LT_GLA // GLA_CHUNK), np.tril(np.ones((GLA_CHUNK, GLA_CHUNK)))), BF16)
    gmean = jnp.asarray(np.kron(np.eye(GLA_HEADS), np.full((GLA_DV, GLA_DV), 1.0 / GLA_DV)), BF16)
    perm = _s5_perm()
    perm_b, permt_b = jnp.asarray(perm, BF16), jnp.asarray(perm.T, BF16)

    ls_e = jnp.repeat(s5_log_step, S5_STATE, axis=1)[:, None, :]
    lr_e = s5_a_re.reshape(L, 1, S5_LANES)
    li_e = s5_a_im.reshape(L, 1, S5_LANES)
    br_e = s5_b_re.transpose(0, 3, 1, 2).reshape(L, S5_CH, S5_LANES)
    bi_e = s5_b_im.transpose(0, 3, 1, 2).reshape(L, S5_CH, S5_LANES)
    bbr, bbi, ab = _s5_prep(ls_e, lr_e, li_e, br_e, bi_e)
    grp = jnp.arange(S5_LANES) // S5_STATE
    gsel = (jnp.arange(S5_GROUPS)[:, None] == grp[None, :]).astype(F32)

    for l in range(L):
        lam_init = 0.8 - 0.6 * math.exp(-0.3 * l)
        wi = w_in[l]
        offs = np.cumsum([0, HK, HK, 2 * HK, GLA_RANK, 2 * HK, W, W, W, W, W, W])
        seg = [wi[:, offs[i]:offs[i + 1]] for i in range(11)]
        g_q, g_k, g_v, g_lr, g_og, d_q, d_k, d_v, s_u, r_x, r_g = seg
        glr_pad = jnp.pad(g_lr, ((0, 0), (0, LANES - GLA_RANK)))
        wn = jnp.concatenate([g_q, g_k, g_v, g_og, glr_pad, d_k, s_u, r_x, r_g], axis=1).astype(BF16)
        wt = jnp.concatenate([d_q, d_v], axis=1).T.astype(BF16)

        gla_p, kd, qt, vt, su, o_d = _inproj(
            x, ln_mix_pre[l][None], wn, wt, cos_n, sin_n, cos_t, sin_t,
            lru_conv_w[l], lru_conv_b[l][None],
            _block_diag(lru_w_a[l]).astype(BF16), lru_b_a[l][None],
            _block_diag(lru_w_x[l]).astype(BF16), lru_b_x[l][None], lru_lambda[l][None])

        wg = jnp.pad(gla_w_gate[l], ((0, LANES - GLA_RANK), (0, 0))).astype(BF16)
        o_a = _gla(gla_p, wg, gla_b_gate[l][None], tril, gmean,
                   jnp.tile(gla_norm[l], GLA_HEADS)[None])

        lq = jnp.stack([diff_lq1[l], diff_lq2[l]])
        lk = jnp.stack([diff_lk1[l], diff_lk2[l]])
        gain_t = jnp.broadcast_to(jnp.tile(diff_norm[l], DIFF_HEADS)[:, None], (W, T_ATT))
        o_b = _diff(qt, kd, vt, lq, lk, gain_t, lam_init)

        bb = jnp.concatenate(
            [(gsel[:, None, :] * b[None, :, :]).reshape(W, S5_LANES) for b in (bbr[l], bbi[l])],
            axis=1).astype(BF16)
        cm = jnp.concatenate([_block_diag(s5_c_re[l]).T, -_block_diag(s5_c_im[l]).T], axis=0).astype(BF16)
        o_c = _s5(su, perm_b, permt_b, bb, cm, ab[l], s5_d[l][None],
                  s5_w_glu[l].astype(BF16), s5_b_glu[l][None])

        x = _outffn(x, o_a, o_b, o_c, o_d, w_out[l].astype(BF16),
                    ln_mix_post[l][None], ln_ffn_pre[l][None], ln_ffn_post[l][None],
                    ffn_w1[l].astype(BF16), ffn_w2[l].astype(BF16))
    return x
```

```python
import functools
import math

import numpy as np
import jax
import jax.numpy as jnp
from jax import lax
from jax.experimental import pallas as pl
from jax.experimental.pallas import tpu as pltpu

F32 = jnp.float32
BF16 = jnp.bfloat16

D_MODEL = 1024
GROUP_WIDTH = 256
EPS = 1e-6
GLA_HEADS, GLA_DK, GLA_DV, GLA_RANK, GLA_CHUNK = 4, 32, 64, 16, 64
GLA_GATE_NORM = 16.0
DIFF_HEADS, DIFF_DH, DIFF_DV = 4, 32, 64
ROPE_THETA = 10000.0
S5_GROUPS, S5_STATE, S5_CH = 16, 64, 16
S5_LANES = S5_GROUPS * S5_STATE
LRU_BLOCKS, LRU_BLOCK, LRU_CONV, LRU_C = 4, 64, 4, 8.0
D_FF = 4 * D_MODEL

SUBLANES = 8
LANES = 128
VMEM_LIMIT = 56 * 1024 * 1024

TM_INPROJ = 512
LRU_SCAN_ROWS = 8
T_ATT = 512
LT_GLA = 256
NB_GLA = 4
TS_S5 = 32
NSUB_S5 = 8
TM_FFN = 512
FFN_SUBTILES = 2
FF_CHUNK = 1024


def _dot(a, b):
    return jnp.dot(a, b, preferred_element_type=F32)


def _dot_nt(a, b):
    return lax.dot_general(a, b, (((1,), (1,)), ((), ())), preferred_element_type=F32)


def _dot_tn(a, b):
    return lax.dot_general(a, b, (((0,), (0,)), ((), ())), preferred_element_type=F32)


def _split2(x):
    hi = x.astype(BF16)
    lo = (x - hi.astype(F32)).astype(BF16)
    return hi, lo


def _split3(x):
    hi = x.astype(BF16)
    r = x - hi.astype(F32)
    mid = r.astype(BF16)
    lo = (r - mid.astype(F32)).astype(BF16)
    return hi, mid, lo


def _sigmoid(x):
    return 1.0 / (1.0 + jnp.exp(-x))


def _softplus(x):
    return jnp.maximum(x, 0.0) + jnp.log1p(jnp.exp(-jnp.abs(x)))


def _gelu_tanh(x):
    c = math.sqrt(2.0 / math.pi)
    return 0.5 * x * (1.0 + jnp.tanh(c * (x + 0.044715 * (x * x * x))))


def _rms(x, g):
    ms = jnp.mean(x * x, axis=-1, keepdims=True)
    return x * lax.rsqrt(ms + EPS) * g


def _params(*sem):
    return pltpu.CompilerParams(dimension_semantics=sem, vmem_limit_bytes=VMEM_LIMIT)


def _const_spec(shape):
    nd = len(shape)
    return pl.BlockSpec(shape, lambda *_: (0,) * nd)


N_GLA = 896
N_NAT = N_GLA + 4 * GROUP_WIDTH
N_T = 2 * GROUP_WIDTH


def _shift_rows(x, d, fill):
    n = x.shape[0]
    if d % SUBLANES == 0:
        return jnp.concatenate([jnp.full((d, x.shape[1]), fill, x.dtype), x[:n - d]], axis=0)
    r = lax.broadcasted_iota(jnp.int32, x.shape, 0)
    return jnp.where(r >= d, pltpu.roll(x, d, 0), fill)


def _inproj_kernel(x_ref, g_ref, wn_ref, wt_ref, cn_ref, sn_ref, ct_ref, st_ref,
                   cw_ref, cb_ref, wa_ref, ba_ref, wx_ref, bx_ref, lam_ref,
                   gla_ref, kd_ref, qt_ref, vt_ref, su_ref, od_ref, tail_ref, hst_ref):
    tm = x_ref.shape[0]
    W = GROUP_WIDTH
    HALF = DIFF_DH // 2

    @pl.when(pl.program_id(1) == 0)
    def _():
        tail_ref[...] = jnp.zeros_like(tail_ref)
        hst_ref[...] = jnp.zeros_like(hst_ref)

    h = _rms(x_ref[...], g_ref[...]).astype(BF16)
    o = N_GLA
    rxg = _dot(h, wn_ref[:, o + 2 * W:o + 4 * W])
    xb, gate = rxg[:, :W], rxg[:, W:]
    ext = jnp.concatenate([tail_ref[...], xb], axis=0)
    tail_ref[...] = xb[tm - SUBLANES:]
    xc = xb * cw_ref[LRU_CONV - 1:LRU_CONV, :] + cb_ref[...]
    for sh in range(1, LRU_CONV):
        xs = pltpu.roll(ext, sh, 0)[SUBLANES:]
        xc = xc + xs * cw_ref[LRU_CONV - 1 - sh:LRU_CONV - sh, :]
    xcb = xc.astype(BF16)

    k = _dot(h, wn_ref[:, o:o + W])
    first_half = lax.broadcasted_iota(jnp.int32, k.shape, 1) % DIFF_DH < HALF
    k_sw = jnp.where(first_half, pltpu.roll(k, W - HALF, 1), pltpu.roll(k, HALF, 1))
    kd_ref[...] = (k * cn_ref[...] + k_sw * sn_ref[...]).astype(BF16)

    r = _sigmoid(_dot(xcb, wa_ref[...]) + ba_ref[...])
    ig = _sigmoid(_dot(xcb, wx_ref[...]) + bx_ref[...])

    gla_ref[...] = _dot(h, wn_ref[:, :N_GLA])
    su_ref[...] = _dot(h, wn_ref[:, o + W:o + 2 * W])
    q = _dot_nt(wt_ref[:W, :], h)
    q_sw = jnp.concatenate(
        [q[g * DIFF_DH + off:g * DIFF_DH + off + HALF] for g in range(W // DIFF_DH) for off in (HALF, 0)],
        axis=0)
    qt_ref[...] = (q * ct_ref[...] + q_sw * st_ref[...]).astype(BF16)
    vt = _dot_nt(wt_ref[W:, :], h).astype(BF16)
    for j in range(vt_ref.shape[0]):
        vt_ref[j] = vt[:, j * T_ATT:(j + 1) * T_ATT]

    sp = _softplus(-lam_ref[...])
    h_prev = hst_ref[...]
    for s0 in range(0, tm, LRU_SCAN_ROWS):
        rows = slice(s0, s0 + LRU_SCAN_ROWS)
        log_a = -LRU_C * r[rows] * sp
        a = jnp.exp(log_a)
        th = jnp.tanh(log_a)
        mult = jnp.sqrt(jnp.maximum(-2.0 * th / (1.0 - th), 1e-12))
        b = mult * (ig[rows] * xc[rows])
        d = 1
        while d < LRU_SCAN_ROWS:
            b = b + a * _shift_rows(b, d, 0.0)
            a = a * _shift_rows(a, d, 1.0)
            d *= 2
        hs = b + a * h_prev
        h_prev = hs[LRU_SCAN_ROWS - 1:LRU_SCAN_ROWS]
        od_ref[rows, :] = (hs * _gelu_tanh(gate[rows])).astype(BF16)
    hst_ref[...] = h_prev


def _inproj(x, g, wn, wt, cn, sn, ct, st, cw, cb, wa, ba, wx, bx, lam):
    B, S, D = x.shape
    tm = TM_INPROJ
    W = GROUP_WIDTH
    nsub = tm // T_ATT
    out_shape = (
        jax.ShapeDtypeStruct((B, S, N_GLA), F32),
        jax.ShapeDtypeStruct((B, S, W), BF16),
        jax.ShapeDtypeStruct((B, W, S), BF16),
        jax.ShapeDtypeStruct((B, S // T_ATT, W, T_ATT), BF16),
        jax.ShapeDtypeStruct((S, B * W), F32),
        jax.ShapeDtypeStruct((B, S, W), BF16),
    )
    row = lambda n: pl.BlockSpec((None, tm, n), lambda b, t: (b, t, 0))
    return pl.pallas_call(
        _inproj_kernel,
        out_shape=out_shape,
        grid=(B, S // tm),
        in_specs=[
            row(D),
            _const_spec((1, D)),
            _const_spec((D, N_NAT)),
            _const_spec((N_T, D)),
            pl.BlockSpec((tm, W), lambda b, t: (t, 0)),
            pl.BlockSpec((tm, W), lambda b, t: (t, 0)),
            pl.BlockSpec((W, tm), lambda b, t: (0, t)),
            pl.BlockSpec((W, tm), lambda b, t: (0, t)),
            _const_spec((LRU_CONV, W)),
            _const_spec((1, W)),
            _const_spec((W, W)),
            _const_spec((1, W)),
            _const_spec((W, W)),
            _const_spec((1, W)),
            _const_spec((1, W)),
        ],
        out_specs=(
            row(N_GLA),
            row(W),
            pl.BlockSpec((None, W, tm), lambda b, t: (b, 0, t)),
            pl.BlockSpec((None, nsub, W, T_ATT), lambda b, t: (b, t, 0, 0)),
            pl.BlockSpec((tm, W), lambda b, t: (t, b)),
            row(W),
        ),
        scratch_shapes=[pltpu.VMEM((SUBLANES, W), F32), pltpu.VMEM((1, W), F32)],
        compiler_params=_params("parallel", "arbitrary"),
        name="inproj",
    )(x, g, wn, wt, cn, sn, ct, st, cw, cb, wa, ba, wx, bx, lam)


def _gla_kernel(p_ref, wg_ref, bg_ref, tril_ref, gm_ref, gain_ref, o_ref, st_ref):
    NB, Lt, _ = p_ref.shape
    C = GLA_CHUNK
    HK = GLA_HEADS * GLA_DK
    HV = GLA_HEADS * GLA_DV

    @pl.when(pl.program_id(1) == 0)
    def _():
        st_ref[...] = jnp.zeros_like(st_ref)

    lane_k = lax.broadcasted_iota(jnp.int32, (C, HK), 1) // GLA_DK
    r_a = lax.broadcasted_iota(jnp.int32, (GLA_HEADS * C, C), 0) % C
    c_a = lax.broadcasted_iota(jnp.int32, (GLA_HEADS * C, C), 1)
    causal = r_a >= c_a
    lane_v = lax.broadcasted_iota(jnp.int32, (C, HV), 1) // GLA_DV
    st_row = lax.broadcasted_iota(jnp.int32, (HV, HK), 0) // GLA_DV
    st_col = lax.broadcasted_iota(jnp.int32, (HV, HK), 1) // GLA_DK
    same_head = st_row == st_col
    tril = tril_ref[...]

    q_dec, k_dec, k_raw, v, bcum = [], [], [], [], []
    for b in range(NB):
        k = p_ref[b, :, HK:2 * HK]
        glr = p_ref[b, :, 2 * HK + 2 * HV:].astype(BF16)
        g = -_softplus(-(_dot(glr, wg_ref[...]) + bg_ref[...])) * (1.0 / GLA_GATE_NORM)
        g_hi, g_lo = _split2(g)
        bc = _dot(tril, g_hi) + _dot(tril, g_lo)
        q_dec.append((p_ref[b, :, 0:HK] * (GLA_DK ** -0.5) * jnp.exp(bc)).astype(BF16))
        k_dec.append((k * jnp.exp(-bc)).astype(BF16))
        k_raw.append(k)
        v.append(p_ref[b, :, 2 * HK:2 * HK + HV].astype(BF16))
        bcum.append(bc)

    outs = [[] for _ in range(NB)]
    rng = range(NB)
    for c in range(Lt // C):
        sl = slice(c * C, (c + 1) * C)
        blast = [bcum[b][sl][C - 1:C, :] for b in rng]
        scores = []
        for b in rng:
            qd = q_dec[b][sl]
            q_big = jnp.concatenate(
                [jnp.where(lane_k == h, qd, jnp.zeros_like(qd)) for h in range(GLA_HEADS)], axis=0)
            scores.append(_dot_nt(q_big, k_dec[b][sl]))
        state = [st_ref[b] for b in rng]
        o_inter = [_dot_nt(q_dec[b][sl], state[b].astype(BF16)) for b in rng]
        kv_t = []
        for b in rng:
            k_st = (k_raw[b][sl] * jnp.exp(blast[b] - bcum[b][sl])).astype(BF16)
            kv_t.append(_dot_tn(v[b][sl], k_st))
        o_all = [_dot(jnp.where(causal, scores[b], 0.0).astype(BF16), v[b][sl]) for b in rng]
        for b in rng:
            o_c = o_inter[b]
            for h in range(GLA_HEADS):
                o_c = o_c + jnp.where(lane_v == h, o_all[b][h * C:(h + 1) * C], 0.0)
            st_ref[b] = state[b] * jnp.exp(blast[b]) + jnp.where(same_head, kv_t[b], 0.0)
            outs[b].append(o_c)
    for b in range(NB):
        o = jnp.concatenate(outs[b], axis=0)
        sq_hi, sq_lo = _split2(o * o)
        ms = _dot(sq_hi, gm_ref[...]) + _dot(sq_lo, gm_ref[...])
        o = o * lax.rsqrt(ms + EPS) * gain_ref[...]
        og = p_ref[b, :, 2 * HK + HV:2 * HK + 2 * HV]
        o_ref[b] = (o * (og * _sigmoid(og))).astype(BF16)


def _gla(p, wg, bg, tril, gm, gain):
    B, S, _ = p.shape
    Lt = LT_GLA
    NB = NB_GLA
    HV = GLA_HEADS * GLA_DV
    HK = GLA_HEADS * GLA_DK
    return pl.pallas_call(
        _gla_kernel,
        out_shape=jax.ShapeDtypeStruct((B, S, HV), BF16),
        grid=(B // NB, S // Lt),
        in_specs=[
            pl.BlockSpec((NB, Lt, N_GLA), lambda b, t: (b, t, 0)),
            _const_spec((HK, HK)),
            _const_spec((1, HK)),
            _const_spec((Lt, Lt)),
            _const_spec((HV, HV)),
            _const_spec((1, HV)),
        ],
        out_specs=pl.BlockSpec((NB, Lt, HV), lambda b, t: (b, t, 0)),
        scratch_shapes=[pltpu.VMEM((NB, HV, HK), F32)],
        compiler_params=_params("parallel", "arbitrary"),
        name="gla",
    )(p, wg, bg, tril, gm, gain)


V_AUG = DIFF_DV + 16

def _diff_kernel(qt_ref, k_ref, vt_ref, lq_ref, lk_ref, gain_ref, o_ref, wq_ref, acc_ref, m_ref, s_ref,
                 mx_ref, *, lam_init):
    T = T_ATT
    NB = 2 * DIFF_HEADS
    i = pl.program_id(1)
    @pl.when((pl.program_id(0) == 0) & (i == 0))
    def _():
        wq_ref[...] = jnp.zeros_like(wq_ref)

    for n in range(NB):
        rows = slice(n * DIFF_DH, (n + 1) * DIFF_DH)
        wq_ref[n // 2, rows, (n % 2) * T:(n % 2 + 1) * T] = qt_ref[rows, :]
    ones = jnp.ones((V_AUG - DIFF_DV, T), BF16)

    def produce(j, h, buf, diagonal):
        kb = k_ref[pl.ds(pl.multiple_of(j * T, T), T), :]
        s = _dot(kb, wq_ref[h])
        if diagonal:
            kk = lax.broadcasted_iota(jnp.int32, s.shape, 0)
            qq = lax.broadcasted_iota(jnp.int32, s.shape, 1)
            s = jnp.where(kk <= jnp.where(qq >= T, qq - T, qq), s, -1e30)
        for c in range(2):
            sc = s[:, c * T:(c + 1) * T]
            s_ref[buf, 2 * h + c] = sc
            mx_ref[buf, 2 * h + c] = jnp.max(sc, axis=0, keepdims=True)

    def block(j, j_next, cons, prod, first=False):
        for h in range(DIFF_HEADS):
            if first:
                produce(j, h, cons, True)
            produce(j_next, h, prod, False)
            va = jnp.concatenate([vt_ref[j, h * DIFF_DV:(h + 1) * DIFF_DV, :], ones], axis=0)
            for c in range(2):
                n = 2 * h + c
                if first:
                    m_new = mx_ref[cons, n]
                    acc_ref[n] = _dot(va, jnp.exp2(s_ref[cons, n] - m_new).astype(BF16))
                else:
                    m = m_ref[n]
                    m_new = jnp.maximum(m, mx_ref[cons, n])
                    p = jnp.exp2(s_ref[cons, n] - m_new).astype(BF16)
                    acc_ref[n] = jnp.exp2(m - m_new) * acc_ref[n] + _dot(va, p)
                m_ref[n] = m_new

    block(i, 0, 0, 1, first=True)

    def body(t, carry):
        block(2 * t, 2 * t + 1, 1, 0)
        block(2 * t + 1, jnp.minimum(2 * t + 2, i - 1), 0, 1)
        return carry

    lax.fori_loop(0, i >> 1, body, 0)

    @pl.when((i & 1) == 1)
    def _():
        block(i - 1, i - 1, 1, 0)

    e = jnp.exp(jnp.sum(lq_ref[...] * lk_ref[...], axis=-1, keepdims=True))
    lam = e[0:1] - e[1:2] + lam_init
    heads = []
    for h in range(DIFF_HEADS):
        a1, a2 = acc_ref[2 * h], acc_ref[2 * h + 1]
        o1 = a1[:DIFF_DV] / a1[DIFF_DV:DIFF_DV + 1]
        o2 = a2[:DIFF_DV] / a2[DIFF_DV:DIFF_DV + 1]
        oh = o1 - lam * o2
        ms = jnp.mean(oh * oh, axis=0, keepdims=True)
        heads.append(oh * lax.rsqrt(ms + EPS))
    ot = jnp.concatenate(heads, axis=0) * gain_ref[...] * (1.0 - lam_init)
    o_ref[...] = ot.T.astype(BF16)


def _diff(qt, k, vt, lq, lk, gain_t, lam_init):
    B, W, S = qt.shape
    T = T_ATT
    return pl.pallas_call(
        functools.partial(_diff_kernel, lam_init=lam_init),
        out_shape=jax.ShapeDtypeStruct((B, S, W), BF16),
        grid=(B, S // T),
        in_specs=[
            pl.BlockSpec((None, W, T), lambda b, i: (b, 0, i)),
            pl.BlockSpec((None, S, W), lambda b, i: (b, 0, 0)),
            pl.BlockSpec((None, S // T, W, T), lambda b, i: (b, 0, 0, 0)),
            _const_spec((2, DIFF_DH)),
            _const_spec((2, DIFF_DH)),
            _const_spec((W, T)),
        ],
        out_specs=pl.BlockSpec((None, T, W), lambda b, i: (b, i, 0)),
        scratch_shapes=[pltpu.VMEM((DIFF_HEADS, W, 2 * T), BF16),
                        pltpu.VMEM((2 * DIFF_HEADS, V_AUG, T), F32),
                        pltpu.VMEM((2 * DIFF_HEADS, 1, T), F32),
                        pltpu.VMEM((2, 2 * DIFF_HEADS, T, T), F32),
                        pltpu.VMEM((2, 2 * DIFF_HEADS, 1, T), F32)],
        compiler_params=_params("arbitrary", "arbitrary"),
        name="diff",
    )(qt, k, vt, lq, lk, gain_t)


def _s5_prep_kernel(ls_ref, lr_ref, li_ref, br_ref, bi_ref,
                    bbr_ref, bbi_ref, ab_ref):
    step = jnp.exp(ls_ref[...])
    lr, li = lr_ref[...], li_ref[...]
    mag = jnp.exp(lr * step)
    ab_re = mag * jnp.cos(li * step)
    ab_im = mag * jnp.sin(li * step)
    den = lr * lr + li * li
    nr, ni = ab_re - 1.0, ab_im
    cr = (nr * lr + ni * li) / den
    ci = (ni * lr - nr * li) / den
    bbr_ref[...] = cr * br_ref[...] - ci * bi_ref[...]
    bbi_ref[...] = cr * bi_ref[...] + ci * br_ref[...]

    ab_ref[0] = jnp.broadcast_to(ab_re, (SUBLANES, S5_LANES))
    ab_ref[1] = jnp.broadcast_to(ab_im, (SUBLANES, S5_LANES))


def _s5_prep(ls_e, lr, li, br_e, bi_e):
    L = ls_e.shape[0]
    N = S5_LANES
    vec = pl.BlockSpec((None, 1, N), lambda l: (l, 0, 0))
    mat = pl.BlockSpec((None, S5_CH, N), lambda l: (l, 0, 0))
    return pl.pallas_call(
        _s5_prep_kernel,
        out_shape=(
            jax.ShapeDtypeStruct((L, S5_CH, N), F32),
            jax.ShapeDtypeStruct((L, S5_CH, N), F32),
            jax.ShapeDtypeStruct((L, 2, SUBLANES, N), F32),
        ),
        grid=(L,),
        in_specs=[vec, vec, vec, mat, mat],
        out_specs=(mat, mat, pl.BlockSpec((None, 2, SUBLANES, N), lambda l: (l, 0, 0, 0))),
        compiler_params=_params("arbitrary"),
        name="s5_prep",
    )(ls_e, lr, li, br_e, bi_e)


def _s5_kernel(u_ref, perm_ref, permt_ref, bb_ref, cm_ref, ab_ref, d_ref, wglu_ref, bglu_ref,
               o_ref, z_ref, st_ref):
    N = S5_LANES
    W = GROUP_WIDTH
    TS = TS_S5
    nsub = u_ref.shape[0] // TS

    @pl.when(pl.program_id(0) == 0)
    def _():
        st_ref[...] = jnp.zeros_like(st_ref)

    def slab(k):
        return jnp.concatenate(
            [u_ref[k * TS:(k + 1) * TS, b * W:(b + 1) * W] for b in range(SUBLANES)], axis=0)

    def project(k):
        u_tb = _dot(perm_ref[...], slab(k).astype(BF16)).astype(BF16)
        z_ref[k] = _dot(u_tb, bb_ref[...])

    def glu_store(k, y):
        gate = _sigmoid(_dot(y.astype(BF16), wglu_ref[...]) + bglu_ref[...])
        out = (y * gate).astype(BF16)
        for b in range(SUBLANES):
            o_ref[k * TS:(k + 1) * TS, b * W:(b + 1) * W] = out[b * TS:(b + 1) * TS]

    ar, ai = ab_ref[0], ab_ref[1]
    project(0)
    if nsub > 1:
        project(1)
    y_pending = None
    for k in range(nsub):
        zk = z_ref.at[k]
        zr, zi = st_ref[0], st_ref[1]
        for t in range(TS):
            rs = slice(t * SUBLANES, (t + 1) * SUBLANES)
            zr, zi = (ar * zr - ai * zi + zk[rs, :N], ar * zi + ai * zr + zk[rs, N:])
            zk[rs, :N] = zr
            zk[rs, N:] = zi
        st_ref[0] = zr
        st_ref[1] = zi

        half = SUBLANES * TS // 2
        y_tb = jnp.concatenate([_dot(zk[:half, :].astype(BF16), cm_ref[...]),
                                _dot(zk[half:, :].astype(BF16), cm_ref[...])], axis=0)
        if k + 2 < nsub:
            project(k + 2)
        if y_pending is not None:
            glu_store(k - 1, y_pending)
        y_hi, y_lo = _split2(y_tb)
        y = _dot(permt_ref[...], y_hi) + _dot(permt_ref[...], y_lo)
        y_pending = _gelu_tanh(y + d_ref[...] * slab(k))
    glu_store(nsub - 1, y_pending)


def _s5(u, perm, permt, bb, cm, ab, d, wglu, bglu):
    S, BW = u.shape
    W = GROUP_WIDTH
    assert BW == SUBLANES * W, "the S5 kernel maps the batch onto the 8 sublanes"
    rows = SUBLANES * TS_S5
    Lt = TS_S5 * NSUB_S5
    N = S5_LANES
    return pl.pallas_call(
        _s5_kernel,
        out_shape=jax.ShapeDtypeStruct((S, BW), BF16),
        grid=(S // Lt,),
        in_specs=[
            pl.BlockSpec((Lt, BW), lambda t: (t, 0)),
            _const_spec((rows, rows)),
            _const_spec((rows, rows)),
            _const_spec((W, 2 * N)),
            _const_spec((2 * N, W)),
            _const_spec((2, SUBLANES, N)),
            _const_spec((1, W)),
            _const_spec((W, W)),
            _const_spec((1, W)),
        ],
        out_specs=pl.BlockSpec((Lt, BW), lambda t: (t, 0)),
        scratch_shapes=[pltpu.VMEM((NSUB_S5, rows, 2 * N), F32), pltpu.VMEM((2, SUBLANES, N), F32)],
        compiler_params=_params("arbitrary"),
        name="s5",
    )(u, perm, permt, bb, cm, ab, d, wglu, bglu)


def _outffn_kernel(x_ref, oa_ref, ob_ref, oc_ref, od_ref, wo_ref, g1_ref, g2_ref, g3_ref,
                   w1_ref, w2_ref, out_ref):
    tm = x_ref.shape[0]
    subs = [slice(s0, s0 + tm // FFN_SUBTILES) for s0 in range(0, tm, tm // FFN_SUBTILES)]
    mix = [_dot(jnp.concatenate([oa_ref[r, :], ob_ref[r, :], oc_ref[r, :], od_ref[r, :]], axis=-1),
                wo_ref[...]) for r in subs]
    x1 = [x_ref[r, :] + _rms(m, g1_ref[...]) for r, m in zip(subs, mix)]
    h = [_rms(v, g2_ref[...]).astype(BF16) for v in x1]
    f = [jnp.zeros_like(v) for v in x1]
    for c in range(D_FF // FF_CHUNK):
        cols = slice(c * FF_CHUNK, (c + 1) * FF_CHUNK)
        a = [jnp.maximum(_dot(hv, w1_ref[:, cols]), 0.0) for hv in h]
        f = [fv + _dot((av * av).astype(BF16), w2_ref[cols, :]) for fv, av in zip(f, a)]
    for r, v, fv in zip(subs, x1, f):
        out_ref[r, :] = v + _rms(fv, g3_ref[...])


def _outffn(x, oa, ob, oc, od, wo, g1, g2, g3, w1, w2):
    B, S, D = x.shape
    tm = TM_FFN
    W = GROUP_WIDTH
    row = lambda n: pl.BlockSpec((None, tm, n), lambda b, t: (b, t, 0))
    once = lambda shape: pl.BlockSpec(shape, lambda b, t: (0, 0), pipeline_mode=pl.Buffered(1))
    return pl.pallas_call(
        _outffn_kernel,
        out_shape=jax.ShapeDtypeStruct((B, S, D), F32),
        grid=(B, S // tm),
        in_specs=[row(D), row(W), row(W), pl.BlockSpec((tm, W), lambda b, t: (t, b)), row(W),
                  once((D, D)), _const_spec((1, D)), _const_spec((1, D)), _const_spec((1, D)),
                  once((D, D_FF)), once((D_FF, D))],
        out_specs=row(D),
        compiler_params=_params("parallel", "parallel"),
        name="outffn",
    )(x, oa, ob, oc, od, wo, g1, g2, g3, w1, w2)


def _rope_tables(S):
    inv = ROPE_THETA ** (-jnp.arange(0, DIFF_DH, 2, dtype=F32) / DIFF_DH)
    ang = jnp.arange(S, dtype=F32)[:, None] * inv[None, :]
    cos = jnp.concatenate([jnp.cos(ang), jnp.cos(ang)], axis=-1)
    sin = jnp.concatenate([-jnp.sin(ang), jnp.sin(ang)], axis=-1)
    reps = GROUP_WIDTH // DIFF_DH
    return jnp.tile(cos, (1, reps)), jnp.tile(sin, (1, reps))


def _block_diag(blocks):
    n, a, b = blocks.shape
    eye = jnp.eye(n, dtype=blocks.dtype)
    return (eye[:, None, :, None] * blocks[:, :, None, :]).reshape(n * a, n * b)


def _s5_perm():
    n = SUBLANES * TS_S5
    r = np.arange(n)
    src = (r % SUBLANES) * TS_S5 + r // SUBLANES
    p = np.zeros((n, n), np.float32)
    p[r, src] = 1.0
    return p


def kernel(x, ln_mix_pre, ln_mix_post, ln_ffn_pre, ln_ffn_post, w_in, w_out, gla_w_gate, gla_b_gate, gla_norm, diff_lq1, diff_lk1, diff_lq2, diff_lk2, diff_norm, s5_log_step, s5_a_re, s5_a_im, s5_b_re, s5_b_im, s5_c_re, s5_c_im, s5_d, s5_w_glu, s5_b_glu, lru_conv_w, lru_conv_b, lru_w_a, lru_b_a, lru_w_x, lru_b_x, lru_lambda, ffn_w1, ffn_w2):
    B, S, D = x.shape
    L = w_in.shape[0]
    W = GROUP_WIDTH
    HK = GLA_HEADS * GLA_DK

    cos_n, sin_n = _rope_tables(S)
    q_scale = DIFF_DH ** -0.5 * math.log2(math.e)
    cos_t = (cos_n * q_scale).T
    sin_t = (sin_n * q_scale).T
    tril = jnp.asarray(np.kron(np.eye(LT_GLA // GLA_CHUNK), np.tril(np.ones((GLA_CHUNK, GLA_CHUNK)))), BF16)
    gmean = jnp.asarray(np.kron(np.eye(GLA_HEADS), np.full((GLA_DV, GLA_DV), 1.0 / GLA_DV)), BF16)
    perm = _s5_perm()
    perm_b, permt_b = jnp.asarray(perm, BF16), jnp.asarray(perm.T, BF16)

    ls_e = jnp.repeat(s5_log_step, S5_STATE, axis=1)[:, None, :]
    lr_e = s5_a_re.reshape(L, 1, S5_LANES)
    li_e = s5_a_im.reshape(L, 1, S5_LANES)
    br_e = s5_b_re.transpose(0, 3, 1, 2).reshape(L, S5_CH, S5_LANES)
    bi_e = s5_b_im.transpose(0, 3, 1, 2).reshape(L, S5_CH, S5_LANES)
    bbr, bbi, ab = _s5_prep(ls_e, lr_e, li_e, br_e, bi_e)
    grp = jnp.arange(S5_LANES) // S5_STATE
    gsel = (jnp.arange(S5_GROUPS)[:, None] == grp[None, :]).astype(F32)

    for l in range(L):
        lam_init = 0.8 - 0.6 * math.exp(-0.3 * l)
        wi = w_in[l]
        offs = np.cumsum([0, HK, HK, 2 * HK, GLA_RANK, 2 * HK, W, W, W, W, W, W])
        seg = [wi[:, offs[i]:offs[i + 1]] for i in range(11)]
        g_q, g_k, g_v, g_lr, g_og, d_q, d_k, d_v, s_u, r_x, r_g = seg
        glr_pad = jnp.pad(g_lr, ((0, 0), (0, LANES - GLA_RANK)))
        wn = jnp.concatenate([g_q, g_k, g_v, g_og, glr_pad, d_k, s_u, r_x, r_g], axis=1).astype(BF16)
        wt = jnp.concatenate([d_q, d_v], axis=1).T.astype(BF16)

        gla_p, kd, qt, vt, su, o_d = _inproj(
            x, ln_mix_pre[l][None], wn, wt, cos_n, sin_n, cos_t, sin_t,
            lru_conv_w[l], lru_conv_b[l][None],
            _block_diag(lru_w_a[l]).astype(BF16), lru_b_a[l][None],
            _block_diag(lru_w_x[l]).astype(BF16), lru_b_x[l][None], lru_lambda[l][None])

        wg = jnp.pad(gla_w_gate[l], ((0, LANES - GLA_RANK), (0, 0))).astype(BF16)
        o_a = _gla(gla_p, wg, gla_b_gate[l][None], tril, gmean,
                   jnp.tile(gla_norm[l], GLA_HEADS)[None])

        lq = jnp.stack([diff_lq1[l], diff_lq2[l]])
        lk = jnp.stack([diff_lk1[l], diff_lk2[l]])
        gain_t = jnp.broadcast_to(jnp.tile(diff_norm[l], DIFF_HEADS)[:, None], (W, T_ATT))
        o_b = _diff(qt, kd, vt, lq, lk, gain_t, lam_init)

        bb = jnp.concatenate(
            [(gsel[:, None, :] * b[None, :, :]).reshape(W, S5_LANES) for b in (bbr[l], bbi[l])],
            axis=1).astype(BF16)
        cm = jnp.concatenate([_block_diag(s5_c_re[l]).T, -_block_diag(s5_c_im[l]).T], axis=0).astype(BF16)
        o_c = _s5(su, perm_b, permt_b, bb, cm, ab[l], s5_d[l][None],
                  s5_w_glu[l].astype(BF16), s5_b_glu[l][None])

        x = _outffn(x, o_a, o_b, o_c, o_d, w_out[l].astype(BF16),
                    ln_mix_post[l][None], ln_ffn_pre[l][None], ln_ffn_post[l][None],
                    ffn_w1[l].astype(BF16), ffn_w2[l].astype(BF16))
    return x
```
